```python
import math, functools
import jax, jax.numpy as jnp
from jax import lax
import numpy as np

D_MODEL = 4096
BATCH = 4
SEQ = 2048
DEPTH = 2
DEC_BATCH = 32
DEC_SEQ = 4
PAST_LEN = 16384
PAGE_SIZE = 128

D_MIX = D_MODEL
D_ATTN = D_MIX // 2
D_SSM = D_MIX - D_ATTN
HEAD_DIM = 64
N_HEADS = D_ATTN // HEAD_DIM
N_KV = max(N_HEADS // 8, 1)
Q_PER_KV = N_HEADS // N_KV
WINDOW = 128
BLOCK = WINDOW
ROT_DIM = HEAD_DIM // 4
ROPE_THETA = 500000.0
SSM_HEAD_DIM = 64
N_SSM_HEADS = D_SSM // SSM_HEAD_DIM
N_SSM_GROUPS = 4
SSM_R = N_SSM_HEADS // N_SSM_GROUPS
D_STATE = 128
CONV_W = 4
CONV_DIM = D_SSM + 2 * N_SSM_GROUPS * D_STATE
CHUNK = 128
D_FF = -(-8 * D_MODEL // (3 * 256)) * 256
EPS = 1e-6

Q_W = N_HEADS * HEAD_DIM
KV_W = N_KV * HEAD_DIM
SPLIT_IDX = (Q_W, Q_W + KV_W, Q_W + 2 * KV_W, Q_W + 2 * KV_W + D_SSM, Q_W + 2 * KV_W + D_SSM + CONV_DIM)
P_IN = SPLIT_IDX[-1] + N_SSM_HEADS

kernel_name = "hymba_swa_sink_ssd_adaln_step"


def rmsnorm(x, w):
    xf = x.astype(jnp.float32)
    y = xf * lax.rsqrt(jnp.mean(xf * xf, axis=-1, keepdims=True) + EPS)
    return (y * w.astype(jnp.float32)).astype(x.dtype)


def rope(x, pos):
    inv = ROPE_THETA ** (-jnp.arange(0, ROT_DIM, 2, dtype=jnp.float32) / ROT_DIM)
    ang = pos.astype(jnp.float32)[:, None] * inv[None, :]
    cos = jnp.cos(ang)[None, :, None, :]
    sin = jnp.sin(ang)[None, :, None, :]
    xr = x[..., :ROT_DIM].astype(jnp.float32)
    x1, x2 = xr[..., :ROT_DIM // 2], xr[..., ROT_DIM // 2:]
    rot = jnp.concatenate([x1 * cos - x2 * sin, x2 * cos + x1 * sin], axis=-1)
    return jnp.concatenate([rot.astype(x.dtype), x[..., ROT_DIM:]], axis=-1)


def sink_softmax(s, sink):
    m = jnp.maximum(jnp.max(s, axis=-1, keepdims=True), sink)
    e = jnp.exp(s - m)
    return e / (jnp.sum(e, axis=-1, keepdims=True) + jnp.exp(sink - m))


def swa_prompt(q, k, v, sinks):
    b, L = q.shape[:2]
    nb = L // BLOCK
    qb = q.reshape(b, nb, BLOCK, N_KV, Q_PER_KV, HEAD_DIM)
    kb = k.reshape(b, nb, BLOCK, N_KV, HEAD_DIM)
    vb = v.reshape(b, nb, BLOCK, N_KV, HEAD_DIM)
    pad = ((0, 0), (1, 0), (0, 0), (0, 0), (0, 0))
    kk = jnp.concatenate([jnp.pad(kb, pad)[:, :-1], kb], axis=2)
    vv = jnp.concatenate([jnp.pad(vb, pad)[:, :-1], vb], axis=2)
    s = jnp.einsum('bnqkgd,bnskd->bnkgqs', qb, kk, preferred_element_type=jnp.float32) * (HEAD_DIM ** -0.5)
    qi = jnp.arange(BLOCK)[:, None] + BLOCK
    sj = jnp.arange(2 * BLOCK)[None, :]
    band = (sj <= qi) & (sj > qi - WINDOW)
    first = (jnp.arange(nb)[:, None, None] > 0) | (sj[None] >= BLOCK)
    mask = (band[None] & first)[:, None, None]
    s = jnp.where(mask[None], s, -jnp.inf)
    p = sink_softmax(s, sinks.astype(jnp.float32).reshape(N_KV, Q_PER_KV)[:, :, None, None])
    o = jnp.einsum('bnkgqs,bnskd->bnqkgd', p.astype(v.dtype), vv).reshape(b, L, Q_W)
    keep = min(WINDOW, L)
    return o, k[:, L - keep:], v[:, L - keep:]


def swa_decode(q, k, v, sinks, k_cache, v_cache):
    b, T = q.shape[:2]
    wc = k_cache.shape[1]
    kk = jnp.concatenate([k_cache.astype(k.dtype), k], axis=1)
    vv = jnp.concatenate([v_cache.astype(v.dtype), v], axis=1)
    kpos = PAST_LEN - wc + jnp.arange(wc + T)
    qpos = PAST_LEN + jnp.arange(T)
    mask = (kpos[None, :] <= qpos[:, None]) & (kpos[None, :] > qpos[:, None] - WINDOW)
    qg = q.reshape(b, T, N_KV, Q_PER_KV, HEAD_DIM)
    s = jnp.einsum('btkgd,bskd->bkgts', qg, kk, preferred_element_type=jnp.float32) * (HEAD_DIM ** -0.5)
    s = jnp.where(mask[None, None, None], s, -jnp.inf)
    p = sink_softmax(s, sinks.astype(jnp.float32).reshape(N_KV, Q_PER_KV)[:, :, None, None])
    o = jnp.einsum('bkgts,bskd->btkgd', p.astype(v.dtype), vv).reshape(b, T, Q_W)
    return o, kk[:, T:], vv[:, T:]


def ssd(x, dt, A, Bm, Cm, h0):
    b, T = x.shape[:2]
    Q = CHUNK if T % CHUNK == 0 else T
    nc = T // Q
    f32 = jnp.float32
    xdt = (x.astype(f32) * dt[..., None]).reshape(b, nc, Q, N_SSM_GROUPS, SSM_R, SSM_HEAD_DIM)
    acum = jnp.cumsum((dt * A).reshape(b, nc, Q, N_SSM_GROUPS, SSM_R), axis=2)
    Bc = Bm.astype(f32).reshape(b, nc, Q, N_SSM_GROUPS, D_STATE)
    Cc = Cm.astype(f32).reshape(b, nc, Q, N_SSM_GROUPS, D_STATE)
    seg = acum[:, :, :, None] - acum[:, :, None, :]
    causal = jnp.tril(jnp.ones((Q, Q), bool))[None, None, :, :, None, None]
    decay = jnp.where(causal, jnp.exp(jnp.where(causal, seg, 0.0)), 0.0)
    cb = jnp.einsum('bclgn,bcsgn->bclsg', Cc, Bc)
    y_diag = jnp.einsum('bclsgr,bcsgrp->bclgrp', cb[..., None] * decay, xdt)
    decay_end = jnp.exp(acum[:, :, -1:] - acum)
    states = jnp.einsum('bcsgn,bcsgrp->bcgrpn', Bc, xdt * decay_end[..., None])
    chunk_decay = jnp.exp(acum[:, :, -1])

    def step(h, inp):
        s_c, d_c = inp
        return h * d_c[..., None, None] + s_c, h

    h_final, h_in = lax.scan(step, h0.astype(f32), (jnp.moveaxis(states, 1, 0), jnp.moveaxis(chunk_decay, 1, 0)))
    h_in = jnp.moveaxis(h_in, 0, 1)
    y_off = jnp.einsum('bclgn,bcgrpn->bclgrp', Cc, h_in) * jnp.exp(acum)[..., None]
    y = (y_diag + y_off).reshape(b, T, N_SSM_GROUPS, SSM_R, SSM_HEAD_DIM)
    return y, h_final


def ssm_branch(z, xbc, dt_raw, conv_prefix, h0, conv_w, conv_b, dt_bias, a_log, d_skip, norm_w):
    b, T = xbc.shape[:2]
    xpad = jnp.concatenate([conv_prefix.astype(xbc.dtype), xbc], axis=1)
    conv = lax.conv_general_dilated(xpad, conv_w.astype(xbc.dtype)[:, None, :], (1,), 'VALID',
                                    dimension_numbers=('NWC', 'WIO', 'NWC'), feature_group_count=CONV_DIM)
    xc = jax.nn.silu(conv + conv_b)
    new_conv = xpad[:, T:]
    xs = xc[..., :D_SSM].reshape(b, T, N_SSM_GROUPS, SSM_R, SSM_HEAD_DIM)
    Bm = xc[..., D_SSM:D_SSM + N_SSM_GROUPS * D_STATE].reshape(b, T, N_SSM_GROUPS, D_STATE)
    Cm = xc[..., D_SSM + N_SSM_GROUPS * D_STATE:].reshape(b, T, N_SSM_GROUPS, D_STATE)
    dt = jax.nn.softplus(dt_raw.astype(jnp.float32) + dt_bias.astype(jnp.float32)).reshape(b, T, N_SSM_GROUPS, SSM_R)
    A = -jnp.exp(a_log.astype(jnp.float32)).reshape(N_SSM_GROUPS, SSM_R)
    y, h_new = ssd(xs, dt, A, Bm, Cm, h0.reshape(b, N_SSM_GROUPS, SSM_R, SSM_HEAD_DIM, D_STATE))
    y = y + d_skip.astype(jnp.float32).reshape(N_SSM_GROUPS, SSM_R)[:, :, None] * xs.astype(jnp.float32)
    g = (y.reshape(b, T, D_SSM) * jax.nn.silu(z.astype(jnp.float32))).reshape(b, T, N_SSM_GROUPS, -1)
    g = g * lax.rsqrt(jnp.mean(g * g, axis=-1, keepdims=True) + EPS)
    out = (g.reshape(b, T, D_SSM) * norm_w.astype(jnp.float32)).astype(z.dtype)
    h_new = h_new.reshape(b, N_SSM_HEADS, SSM_HEAD_DIM, D_STATE).astype(z.dtype)
    return out, new_conv, h_new


def trunk_layer(x, c, pos, attn_fn, conv_prefix, h0, w_ada, b_ada, norm1_w, w_in, conv_w, conv_b, dt_bias,
                a_log, d_skip, ssm_norm_w, sinks, w_out, norm2_w, w_gate, w_up, w_down):
    b, T, _ = x.shape
    mod = jax.nn.silu(c) @ w_ada + b_ada
    sh1, sc1, g1, sh2, sc2, g2 = [m[:, None] for m in jnp.split(mod, 6, axis=-1)]
    h = rmsnorm(x, norm1_w) * (1 + sc1) + sh1
    q, k, v, z, xbc, dt_raw = jnp.split(h @ w_in, SPLIT_IDX, axis=-1)
    q = rope(q.reshape(b, T, N_HEADS, HEAD_DIM), pos)
    k = rope(k.reshape(b, T, N_KV, HEAD_DIM), pos)
    v = v.reshape(b, T, N_KV, HEAD_DIM)
    o_attn, k_keep, v_keep = attn_fn(q, k, v, sinks)
    o_ssm, conv_new, h_new = ssm_branch(z, xbc, dt_raw, conv_prefix, h0, conv_w, conv_b, dt_bias, a_log,
                                        d_skip, ssm_norm_w)
    x = x + g1 * (jnp.concatenate([o_attn, o_ssm], axis=-1) @ w_out)
    h2 = rmsnorm(x, norm2_w) * (1 + sc2) + sh2
    x = x + g2 * ((jax.nn.silu(h2 @ w_gate) * (h2 @ w_up)) @ w_down)
    return x, k_keep, v_keep, conv_new, h_new


def setup_inputs(seed: int = 0) -> dict:
    key = jax.random.key(seed)
    ks = jax.random.split(key, 26)
    f32 = jnp.float32
    win_c = min(WINDOW, PAST_LEN)

    def nrm(k, shape, scale):
        return jax.random.normal(k, shape, f32) * scale

    dt0 = jnp.exp(jax.random.uniform(ks[14], (DEPTH, N_SSM_HEADS), f32, math.log(1e-3), math.log(1e-1)))
    return {
        'x_prompt': nrm(ks[0], (BATCH, SEQ, D_MODEL), 1.0),
        'x_sample': nrm(ks[1], (DEC_BATCH, DEC_SEQ, D_MODEL), 1.0),
        'cache_k': nrm(ks[2], (DEPTH, DEC_BATCH, win_c, N_KV, HEAD_DIM), 1.0),
        'cache_v': nrm(ks[3], (DEPTH, DEC_BATCH, win_c, N_KV, HEAD_DIM), 1.0),
        'state_conv': nrm(ks[4], (DEPTH, DEC_BATCH, CONV_W - 1, CONV_DIM), 1.0),
        'state_ssm': nrm(ks[5], (DEPTH, DEC_BATCH, N_SSM_HEADS, SSM_HEAD_DIM, D_STATE), 0.1),
        'c_prompt': nrm(ks[6], (BATCH, D_MODEL), 1.0),
        'c_sample': nrm(ks[7], (DEC_BATCH, D_MODEL), 1.0),
        'w_ada': nrm(ks[8], (DEPTH, D_MODEL, 6 * D_MODEL), D_MODEL ** -0.5),
        'b_ada': nrm(ks[9], (DEPTH, 6 * D_MODEL), 0.02),
        'norm1_w': 1.0 + nrm(ks[10], (DEPTH, D_MODEL), 0.02),
        'w_in': nrm(ks[11], (DEPTH, D_MODEL, P_IN), D_MODEL ** -0.5),
        'conv_w': nrm(ks[12], (DEPTH, CONV_W, CONV_DIM), CONV_W ** -0.5),
        'conv_b': nrm(ks[13], (DEPTH, CONV_DIM), 0.02),
        'dt_bias': dt0 + jnp.log(-jnp.expm1(-dt0)),
        'a_log': jnp.log(jax.random.uniform(ks[15], (DEPTH, N_SSM_HEADS), f32, 1.0, 16.0)),
        'd_skip': 1.0 + nrm(ks[16], (DEPTH, N_SSM_HEADS), 0.1),
        'ssm_norm_w': 1.0 + nrm(ks[17], (DEPTH, D_SSM), 0.02),
        'sinks': nrm(ks[18], (DEPTH, N_HEADS), 0.5),
        'w_out': nrm(ks[19], (DEPTH, D_MIX, D_MODEL), D_MIX ** -0.5),
        'norm2_w': 1.0 + nrm(ks[20], (DEPTH, D_MODEL), 0.02),
        'w_gate': nrm(ks[21], (DEPTH, D_MODEL, D_FF), D_MODEL ** -0.5),
        'w_up': nrm(ks[22], (DEPTH, D_MODEL, D_FF), D_MODEL ** -0.5),
        'w_down': nrm(ks[23], (DEPTH, D_FF, D_MODEL), D_FF ** -0.5),
        'final_norm_w': 1.0 + nrm(ks[24], (D_MODEL,), 0.02),
    }


def reference(x_prompt, x_sample, cache_k, cache_v, state_conv, state_ssm, c_prompt, c_sample, w_ada, b_ada,
              norm1_w, w_in, conv_w, conv_b, dt_bias, a_log, d_skip, ssm_norm_w, sinks, w_out, norm2_w,
              w_gate, w_up, w_down, final_norm_w):
    bp, tp = x_prompt.shape[:2]
    pos_p = jnp.arange(tp, dtype=jnp.int32)
    pos_s = PAST_LEN + jnp.arange(x_sample.shape[1], dtype=jnp.int32)
    conv0 = jnp.zeros((bp, CONV_W - 1, CONV_DIM), x_prompt.dtype)
    h0 = jnp.zeros((bp, N_SSM_HEADS, SSM_HEAD_DIM, D_STATE), jnp.float32)
    hp, hs = x_prompt, x_sample
    kp, vp, cvp, sp, kd, vd, cvd, sd = [], [], [], [], [], [], [], []
    for l in range(DEPTH):
        lw = (w_ada[l], b_ada[l], norm1_w[l], w_in[l], conv_w[l], conv_b[l], dt_bias[l], a_log[l], d_skip[l],
              ssm_norm_w[l], sinks[l], w_out[l], norm2_w[l], w_gate[l], w_up[l], w_down[l])
        hp, k_, v_, cv_, s_ = trunk_layer(hp, c_prompt, pos_p, swa_prompt, conv0, h0, *lw)
        kp.append(k_); vp.append(v_); cvp.append(cv_); sp.append(s_)
        dec_attn = functools.partial(swa_decode, k_cache=cache_k[l], v_cache=cache_v[l])
        hs, k_, v_, cv_, s_ = trunk_layer(hs, c_sample, pos_s, dec_attn, state_conv[l], state_ssm[l], *lw)
        kd.append(k_); vd.append(v_); cvd.append(cv_); sd.append(s_)
    y_prompt = rmsnorm(hp, final_norm_w)
    y_sample = rmsnorm(hs, final_norm_w)
    return (y_prompt, y_sample, jnp.stack(kp), jnp.stack(vp), jnp.stack(cvp), jnp.stack(sp),
            jnp.stack(kd), jnp.stack(vd), jnp.stack(cvd), jnp.stack(sd))
```

```python
import functools
import math

import jax
import jax.numpy as jnp
from jax import lax
from jax.experimental import pallas as pl
from jax.experimental.pallas import tpu as pltpu

F32 = jnp.float32
BF16 = jnp.bfloat16

D_MODEL = 4096
HEAD_DIM = 64
N_HEADS = 32
N_KV = 4
Q_PER_KV = N_HEADS // N_KV
WINDOW = 128
ROT_DIM = 16
ROPE_THETA = 500000.0
PAST_LEN = 16384
D_SSM = 2048
N_SSM_HEADS = 32
SSM_HEAD_DIM = 64
N_SSM_GROUPS = 4
SSM_R = N_SSM_HEADS // N_SSM_GROUPS
D_STATE = 128
CONV_W = 4
CONV_DIM = D_SSM + 2 * N_SSM_GROUPS * D_STATE
CHUNK = 128
D_FF = 11008
EPS = 1e-6
Q_W = N_HEADS * HEAD_DIM
KV_W = N_KV * HEAD_DIM
OFF_K = Q_W
OFF_V = Q_W + KV_W
OFF_Z = Q_W + 2 * KV_W
OFF_XBC = OFF_Z + D_SSM
OFF_DT = OFF_XBC + CONV_DIM
P_IN = OFF_DT + N_SSM_HEADS

LANES = 128
SUBLANES = 8
VMEM_LIMIT_BYTES = 56 * 1024 * 1024

SAMPLE_ROWS = 8
ADA_ROWS = 40


def _cparams(n_axes):
    return pltpu.CompilerParams(dimension_semantics=("arbitrary",) * n_axes, vmem_limit_bytes=VMEM_LIMIT_BYTES)


def _silu(x):
    return x * jax.nn.sigmoid(x)


def _ada_kernel(c_ref, w_ref, b_ref, o_ref):
    x = _silu(c_ref[...]).astype(BF16)
    w = w_ref[...].astype(BF16)
    o_ref[...] = jnp.dot(x, w, preferred_element_type=F32) + b_ref[...]


def ada_mod(c_all, w_ada, b_ada, tn=1024):
    depth, d, n = w_ada.shape
    rows = c_all.shape[0]
    return pl.pallas_call(
        _ada_kernel,
        grid=(depth, n // tn),
        in_specs=[
            pl.BlockSpec((rows, d), lambda l, j: (0, 0)),
            pl.BlockSpec((None, d, tn), lambda l, j: (l, 0, j)),
            pl.BlockSpec((None, 1, tn), lambda l, j: (l, 0, j)),
        ],
        out_specs=pl.BlockSpec((None, rows, tn), lambda l, j: (l, 0, j)),
        out_shape=jax.ShapeDtypeStruct((depth, rows, n), F32),
        compiler_params=_cparams(2),
        name="ada_mod",
    )(c_all, w_ada, b_ada.reshape(depth, 1, n))


def _norm_mod_kernel(x_ref, nw_ref, sc_ref, sh_ref, o_ref):
    x = x_ref[...]
    y = x * lax.rsqrt(jnp.mean(x * x, axis=-1, keepdims=True) + EPS) * nw_ref[...]
    o_ref[...] = (y * (1.0 + sc_ref[...]) + sh_ref[...]).astype(o_ref.dtype)


def _norm_kernel(x_ref, nw_ref, o_ref):
    x = x_ref[...]
    o_ref[...] = x * lax.rsqrt(jnp.mean(x * x, axis=-1, keepdims=True) + EPS) * nw_ref[...]


def _row_tile(t, cap):
    tt = min(t, cap)
    assert t % tt == 0
    return tt


def norm_mod(x, nw, sc, sh, tt_cap=256):
    b, t, d = x.shape
    tt = _row_tile(t, tt_cap)
    mt = sc.shape[1]
    mod_spec = (pl.BlockSpec((None, 1, d), lambda i, j: (i, 0, 0)) if mt == 1
                else pl.BlockSpec((None, tt, d), lambda i, j: (i, j, 0)))
    return pl.pallas_call(
        _norm_mod_kernel,
        grid=(b, t // tt),
        in_specs=[pl.BlockSpec((None, tt, d), lambda i, j: (i, j, 0)),
                  pl.BlockSpec((1, d), lambda i, j: (0, 0)), mod_spec, mod_spec],
        out_specs=pl.BlockSpec((None, tt, d), lambda i, j: (i, j, 0)),
        out_shape=jax.ShapeDtypeStruct((b, t, d), BF16),
        compiler_params=_cparams(2),
        name="norm_mod",
    )(x, nw.reshape(1, d), sc, sh)


def final_norm(x, nw, tt_cap=256):
    b, t, d = x.shape
    tt = _row_tile(t, tt_cap)
    return pl.pallas_call(
        _norm_kernel,
        grid=(b, t // tt),
        in_specs=[pl.BlockSpec((None, tt, d), lambda i, j: (i, j, 0)),
                  pl.BlockSpec((1, d), lambda i, j: (0, 0))],
        out_specs=pl.BlockSpec((None, tt, d), lambda i, j: (i, j, 0)),
        out_shape=jax.ShapeDtypeStruct((b, t, d), F32),
        compiler_params=_cparams(2),
        name="final_norm",
    )(x, nw.reshape(1, d))


def _first_row_tile():
    return jnp.logical_and(pl.program_id(1) == 0, pl.program_id(2) == 0)


def _inproj_kernel(h_ref, w_ref, o_ref, wbf):
    @pl.when(_first_row_tile())
    def _():
        wbf[...] = w_ref[...].astype(BF16)

    o_ref[...] = jnp.dot(h_ref[...], wbf[...], preferred_element_type=F32)


def mm_inproj(h, w, tm_cap=1024, tn=512):
    b, t, k = h.shape
    n = w.shape[1]
    tm = _row_tile(t, tm_cap)
    return pl.pallas_call(
        _inproj_kernel,
        grid=(pl.cdiv(n, tn), b, t // tm),
        in_specs=[pl.BlockSpec((None, tm, k), lambda j, i, m: (i, m, 0)),
                  pl.BlockSpec((k, tn), lambda j, i, m: (0, j))],
        out_specs=pl.BlockSpec((None, tm, tn), lambda j, i, m: (i, m, j)),
        out_shape=jax.ShapeDtypeStruct((b, t, n), F32),
        scratch_shapes=[pltpu.VMEM((k, tn), BF16)],
        compiler_params=_cparams(3),
        name="mm_inproj",
    )(h, w)


def _outproj_kernel(oa_ref, os_ref, w_ref, x_ref, g_ref, o_ref, wbf):
    @pl.when(_first_row_tile())
    def _():
        wbf[...] = w_ref[...].astype(BF16)

    ka = oa_ref.shape[-1]
    acc = jnp.dot(oa_ref[...], wbf[:ka, :], preferred_element_type=F32)
    acc = acc + jnp.dot(os_ref[...], wbf[ka:, :], preferred_element_type=F32)
    o_ref[...] = x_ref[...] + g_ref[...] * acc


def _gate_spec(g, tm, tn, three_axes_n_first=True):
    if g.shape[1] == 1:
        return pl.BlockSpec((None, 1, tn), lambda j, i, m: (i, 0, j))
    return pl.BlockSpec((None, tm, tn), lambda j, i, m: (i, m, j))


def mm_outproj(oa, os_, w, x, g, tm_cap=1024, tn=512):
    b, t, ka = oa.shape
    k, n = w.shape
    tm = _row_tile(t, tm_cap)
    return pl.pallas_call(
        _outproj_kernel,
        grid=(n // tn, b, t // tm),
        in_specs=[pl.BlockSpec((None, tm, ka), lambda j, i, m: (i, m, 0)),
                  pl.BlockSpec((None, tm, k - ka), lambda j, i, m: (i, m, 0)),
                  pl.BlockSpec((k, tn), lambda j, i, m: (0, j)),
                  pl.BlockSpec((None, tm, tn), lambda j, i, m: (i, m, j)),
                  _gate_spec(g, tm, tn)],
        out_specs=pl.BlockSpec((None, tm, tn), lambda j, i, m: (i, m, j)),
        out_shape=jax.ShapeDtypeStruct((b, t, n), F32),
        scratch_shapes=[pltpu.VMEM((k, tn), BF16)],
        compiler_params=_cparams(3),
        name="mm_outproj",
    )(oa, os_, w, x, g)


def _gateup_kernel(h_ref, wg_ref, wu_ref, o_ref, wgbf, wubf):
    @pl.when(_first_row_tile())
    def _():
        wgbf[...] = wg_ref[...].astype(BF16)
        wubf[...] = wu_ref[...].astype(BF16)

    h = h_ref[...]
    gate = jnp.dot(h, wgbf[...], preferred_element_type=F32)
    up = jnp.dot(h, wubf[...], preferred_element_type=F32)
    o_ref[...] = (_silu(gate) * up).astype(o_ref.dtype)


def mm_gateup(h, wg, wu, tm_cap=1024, tn=256):
    b, t, k = h.shape
    n = wg.shape[1]
    assert n % tn == 0
    tm = _row_tile(t, tm_cap)
    wspec = pl.BlockSpec((k, tn), lambda j, i, m: (0, j))
    return pl.pallas_call(
        _gateup_kernel,
        grid=(n // tn, b, t // tm),
        in_specs=[pl.BlockSpec((None, tm, k), lambda j, i, m: (i, m, 0)), wspec, wspec],
        out_specs=pl.BlockSpec((None, tm, tn), lambda j, i, m: (i, m, j)),
        out_shape=jax.ShapeDtypeStruct((b, t, n), BF16),
        scratch_shapes=[pltpu.VMEM((k, tn), BF16), pltpu.VMEM((k, tn), BF16)],
        compiler_params=_cparams(3),
        name="mm_gateup",
    )(h, wg, wu)


def _cast_kernel(w_ref, o_ref):
    o_ref[...] = w_ref[...].astype(o_ref.dtype)


def cast_bf16(w, tr=256):
    depth, r, c = w.shape
    assert r % tr == 0
    return pl.pallas_call(
        _cast_kernel,
        grid=(depth, r // tr),
        in_specs=[pl.BlockSpec((None, tr, c), lambda l, i: (l, i, 0))],
        out_specs=pl.BlockSpec((None, tr, c), lambda l, i: (l, i, 0)),
        out_shape=jax.ShapeDtypeStruct(w.shape, BF16),
        compiler_params=_cparams(2),
        name="cast_bf16",
    )(w)


def _down_kernel(h_ref, w_ref, x_ref, g_ref, o_ref):
    acc = jnp.dot(h_ref[...], w_ref[...], preferred_element_type=F32)
    o_ref[...] = x_ref[...] + g_ref[...] * acc


def mm_down(hid, wbf, x, g, tm_cap=512, tn=512):
    b, t, k = hid.shape
    n = wbf.shape[1]
    tm = _row_tile(t, tm_cap)
    gspec = (pl.BlockSpec((None, 1, tn), lambda i, m, j: (i, 0, j)) if g.shape[1] == 1
             else pl.BlockSpec((None, tm, tn), lambda i, m, j: (i, m, j)))
    return pl.pallas_call(
        _down_kernel,
        grid=(b, t // tm, n // tn),
        in_specs=[pl.BlockSpec((None, tm, k), lambda i, m, j: (i, m, 0)),
                  pl.BlockSpec((k, tn), lambda i, m, j: (0, j)),
                  pl.BlockSpec((None, tm, tn), lambda i, m, j: (i, m, j)),
                  gspec],
        out_specs=pl.BlockSpec((None, tm, tn), lambda i, m, j: (i, m, j)),
        out_shape=jax.ShapeDtypeStruct((b, t, n), F32),
        compiler_params=_cparams(3),
        name="mm_down",
    )(hid, wbf, x, g)


def rope_tables(pos):
    half = ROT_DIM // 2
    inv = ROPE_THETA ** (-jnp.arange(0, ROT_DIM, 2, dtype=F32) / ROT_DIM)
    ang = pos.astype(F32)[:, None] * inv[None, :]
    cos, sin = jnp.cos(ang), jnp.sin(ang)
    t = pos.shape[0]
    rest = HEAD_DIM - ROT_DIM
    c = jnp.concatenate([cos, cos, jnp.ones((t, rest), F32)], axis=1)
    s_up = jnp.concatenate([-sin, jnp.zeros((t, half + rest), F32)], axis=1)
    s_dn = jnp.concatenate([jnp.zeros((t, half), F32), sin, jnp.zeros((t, rest), F32)], axis=1)
    reps = LANES // HEAD_DIM
    return tuple(jnp.tile(a, (1, reps)) for a in (c, s_up, s_dn))


def _rope(x, c, s_up, s_dn):
    w = x.shape[-1]
    reps = w // LANES
    half = ROT_DIM // 2
    tile = lambda a: jnp.concatenate([a] * reps, axis=1) if reps > 1 else a
    return (x * tile(c) + pltpu.roll(x, w - half, 1) * tile(s_up) + pltpu.roll(x, half, 1) * tile(s_dn))


def _swa_prompt_kernel(sinks_ref, q_ref, kv_ref, c_ref, su_ref, sd_ref, o_ref, nk_ref, nv_ref, kprev, vprev):
    n = pl.program_id(1)
    last = pl.num_programs(1) - 1
    blk = q_ref.shape[0]

    @pl.when(n == 0)
    def _():
        kprev[...] = jnp.zeros_like(kprev)
        vprev[...] = jnp.zeros_like(vprev)

    c, su, sd = c_ref[...], su_ref[...], sd_ref[...]
    q = (_rope(q_ref[...], c, su, sd) * (HEAD_DIM ** -0.5)).astype(BF16)
    kv = kv_ref[...]
    k = _rope(kv[:, :KV_W], c, su, sd)
    v = kv[:, KV_W:]

    @pl.when(n == last)
    def _():
        nk_ref[...] = k
        nv_ref[...] = v

    kb, vb = k.astype(BF16), v.astype(BF16)
    kk = jnp.concatenate([kprev[(n + 1) % 2].astype(BF16), kb], axis=0)
    vv = jnp.concatenate([vprev[(n + 1) % 2].astype(BF16), vb], axis=0)
    qi = lax.broadcasted_iota(jnp.int32, (blk, 2 * blk), 0)
    sj = lax.broadcasted_iota(jnp.int32, (blk, 2 * blk), 1)
    in_prev = jnp.logical_and(jnp.logical_and(sj < blk, sj > qi), n > 0)
    mask = jnp.logical_or(in_prev, jnp.logical_and(sj >= blk, sj - blk <= qi))
    for g in range(N_KV):
        kg = kk[:, g * HEAD_DIM:(g + 1) * HEAD_DIM]
        vg = vv[:, g * HEAD_DIM:(g + 1) * HEAD_DIM]
        for hh in range(Q_PER_KV):
            h = g * Q_PER_KV + hh
            qh = q[:, h * HEAD_DIM:(h + 1) * HEAD_DIM]
            s = lax.dot_general(qh, kg, (((1,), (1,)), ((), ())), preferred_element_type=F32)
            s = jnp.where(mask, s, -jnp.inf)
            sink = sinks_ref[h]
            m = jnp.maximum(jnp.max(s, axis=-1, keepdims=True), sink)
            e = jnp.exp(s - m)
            p = e / (jnp.sum(e, axis=-1, keepdims=True) + jnp.exp(sink - m))
            oh = jnp.dot(p.astype(BF16), vg, preferred_element_type=F32)
            o_ref[:, h * HEAD_DIM:(h + 1) * HEAD_DIM] = oh.astype(o_ref.dtype)
    kprev[n % 2] = k
    vprev[n % 2] = v


def swa_prompt(proj, sinks, tables):
    b, t, _ = proj.shape
    blk = WINDOW
    nb = t // blk
    kv_col = OFF_K // (2 * KV_W)
    tab_spec = pl.BlockSpec((blk, LANES), lambda i, n: (n, 0))
    keep_spec = pl.BlockSpec((None, blk, KV_W), lambda i, n: (i, 0, 0))
    return pl.pallas_call(
        _swa_prompt_kernel,
        grid=(b, nb),
        in_specs=[pl.BlockSpec(memory_space=pltpu.SMEM),
                  pl.BlockSpec((None, blk, Q_W), lambda i, n: (i, n, 0)),
                  pl.BlockSpec((None, blk, 2 * KV_W), lambda i, n: (i, n, kv_col)),
                  tab_spec, tab_spec, tab_spec],
        out_specs=[pl.BlockSpec((None, blk, Q_W), lambda i, n: (i, n, 0)), keep_spec, keep_spec],
        out_shape=[jax.ShapeDtypeStruct((b, t, Q_W), BF16),
                   jax.ShapeDtypeStruct((b, blk, KV_W), F32),
                   jax.ShapeDtypeStruct((b, blk, KV_W), F32)],
        scratch_shapes=[pltpu.VMEM((2, blk, KV_W), F32), pltpu.VMEM((2, blk, KV_W), F32)],
        compiler_params=_cparams(2),
        name="swa_prompt",
    )(sinks, proj, proj, *tables)


def _swa_decode_kernel(t_new, sinks_ref, q_ref, kv_ref, kc_ref, vc_ref, c_ref, su_ref, sd_ref,
                       o_ref, nk_ref, nv_ref, kk, vv):
    rows = q_ref.shape[0]
    wc = kc_ref.shape[0]
    c, su, sd = c_ref[...], su_ref[...], sd_ref[...]
    q = (_rope(q_ref[...], c, su, sd) * (HEAD_DIM ** -0.5)).astype(BF16)
    kv = kv_ref[...]
    k = _rope(kv[:, :KV_W], c, su, sd)
    v = kv[:, KV_W:]
    kc, vc = kc_ref[...], vc_ref[...]

    nk_ref[0:wc - t_new, :] = kc[t_new:, :]
    nv_ref[0:wc - t_new, :] = vc[t_new:, :]
    nk_ref[wc - t_new:wc, :] = k[0:t_new, :]
    nv_ref[wc - t_new:wc, :] = v[0:t_new, :]

    kk[...] = jnp.zeros_like(kk)
    vv[...] = jnp.zeros_like(vv)
    kk[0:wc, :] = kc.astype(BF16)
    vv[0:wc, :] = vc.astype(BF16)
    kk[wc:wc + rows, :] = k.astype(BF16)
    vv[wc:wc + rows, :] = v.astype(BF16)

    nkeys = kk.shape[0]
    nrow = Q_PER_KV * rows
    r = lax.broadcasted_iota(jnp.int32, (nrow, nkeys), 0)
    sj = lax.broadcasted_iota(jnp.int32, (nrow, nkeys), 1)
    tq = r % rows
    mask = jnp.logical_and(jnp.logical_and(sj <= tq + wc, sj > tq + wc - WINDOW), sj < wc + t_new)
    rcol = lax.broadcasted_iota(jnp.int32, (nrow, 1), 0) // rows
    for g in range(N_KV):
        kg = kk[:, g * HEAD_DIM:(g + 1) * HEAD_DIM]
        vg = vv[:, g * HEAD_DIM:(g + 1) * HEAD_DIM]
        qs = jnp.concatenate([q[:, (g * Q_PER_KV + hh) * HEAD_DIM:(g * Q_PER_KV + hh + 1) * HEAD_DIM]
                              for hh in range(Q_PER_KV)], axis=0)
        sink = jnp.zeros((nrow, 1), F32)
        for hh in range(Q_PER_KV):
            sink = jnp.where(rcol == hh, sinks_ref[g * Q_PER_KV + hh], sink)
        s = lax.dot_general(qs, kg, (((1,), (1,)), ((), ())), preferred_element_type=F32)
        s = jnp.where(mask, s, -jnp.inf)
        m = jnp.maximum(jnp.max(s, axis=-1, keepdims=True), sink)
        e = jnp.exp(s - m)
        p = e / (jnp.sum(e, axis=-1, keepdims=True) + jnp.exp(sink - m))
        og = jnp.dot(p.astype(BF16), vg, preferred_element_type=F32)
        for hh in range(Q_PER_KV):
            h = g * Q_PER_KV + hh
            o_ref[:, h * HEAD_DIM:(h + 1) * HEAD_DIM] = og[hh * rows:(hh + 1) * rows, :].astype(o_ref.dtype)


def swa_decode(proj, sinks, cache_k, cache_v, tables, t_new):
    b, rows, _ = proj.shape
    wc = cache_k.shape[1]
    kv_col = OFF_K // (2 * KV_W)
    tab_spec = pl.BlockSpec((rows, LANES), lambda i: (0, 0))
    cache_spec = pl.BlockSpec((None, wc, KV_W), lambda i: (i, 0, 0))
    nkeys = 2 * WINDOW
    return pl.pallas_call(
        functools.partial(_swa_decode_kernel, t_new),
        grid=(b,),
        in_specs=[pl.BlockSpec(memory_space=pltpu.SMEM),
                  pl.BlockSpec((None, rows, Q_W), lambda i: (i, 0, 0)),
                  pl.BlockSpec((None, rows, 2 * KV_W), lambda i: (i, 0, kv_col)),
                  cache_spec, cache_spec, tab_spec, tab_spec, tab_spec],
        out_specs=[pl.BlockSpec((None, rows, Q_W), lambda i: (i, 0, 0)), cache_spec, cache_spec],
        out_shape=[jax.ShapeDtypeStruct((b, rows, Q_W), BF16),
                   jax.ShapeDtypeStruct((b, wc, KV_W), F32),
                   jax.ShapeDtypeStruct((b, wc, KV_W), F32)],
        scratch_shapes=[pltpu.VMEM((nkeys, KV_W), BF16), pltpu.VMEM((nkeys, KV_W), BF16)],
        compiler_params=_cparams(1),
        name="swa_decode",
    )(sinks, proj, proj, cache_k, cache_v, *tables)


def _split3(x):
    hi = x.astype(BF16)
    r1 = x - hi.astype(F32)
    mid = r1.astype(BF16)
    lo = (r1 - mid.astype(F32)).astype(BF16)
    return hi, mid, lo


def _ssd_kernel(valid, dskip_ref, proj_ref, cinit_ref, h0_ref, cw_ref, cb_ref, dtb_ref, alog_ref, nw_ref,
                o_ref, tail_ref, hn_ref, xp, ybuf):
    cidx = pl.program_id(1)
    qin = proj_ref.shape[0]
    q = ybuf.shape[0]
    pad = q - qin
    hist = SUBLANES

    @pl.when(cidx == 0)
    def _():
        xp[0:hist, :] = cinit_ref[...]
        hn_ref[...] = h0_ref[...]

    blk = proj_ref[...]
    xp[hist:hist + qin, :] = blk[:, OFF_XBC:OFF_DT]
    if pad:
        xp[hist + qin:, :] = jnp.zeros((pad, CONV_DIM), F32)
    tail_ref[...] = xp[valid:valid + hist, :]

    cw = cw_ref[...]
    conv = cb_ref[...] + cw[CONV_W - 1:CONV_W, :] * xp[hist:hist + q, :]
    for j in range(CONV_W - 1):
        shift = CONV_W - 1 - j
        conv = conv + cw[j:j + 1, :] * xp[hist - shift:hist - shift + q, :]
    xc = _silu(conv)
    xp[0:hist, :] = xp[q:q + hist, :]

    xs = xc[:, :D_SSM]
    gw = D_STATE
    bmat = xc[:, D_SSM:D_SSM + N_SSM_GROUPS * gw].astype(BF16)
    cmat = xc[:, D_SSM + N_SSM_GROUPS * gw:].astype(BF16)

    dt_raw = blk[:, OFF_DT:OFF_DT + N_SSM_HEADS]
    if pad:
        dt_raw = jnp.concatenate([dt_raw, jnp.zeros((pad, N_SSM_HEADS), F32)], axis=0)
    xdtb = dt_raw + dtb_ref[...]
    dt = jnp.maximum(xdtb, 0.0) + jnp.log1p(jnp.exp(-jnp.abs(xdtb)))
    rowi = lax.broadcasted_iota(jnp.int32, (q, N_SSM_HEADS), 0)
    dt = jnp.where(rowi < valid, dt, 0.0)
    a = dt * (-jnp.exp(alog_ref[...]))

    li = lax.broadcasted_iota(jnp.int32, (q, q), 0)
    si = lax.broadcasted_iota(jnp.int32, (q, q), 1)
    causal = li >= si
    tri = causal.astype(BF16)
    triu = (li <= si).astype(BF16)
    acum = jnp.zeros((q, N_SSM_HEADS), F32)
    acum_t = jnp.zeros((N_SSM_HEADS, q), F32)
    for part in _split3(a):
        acum = acum + jnp.dot(tri, part, preferred_element_type=F32)
        acum_t = acum_t + lax.dot_general(part, triu, (((0,), (0,)), ((), ())), preferred_element_type=F32)
    alast = acum[q - 1:q, :]
    e_acum = jnp.exp(acum)
    e_end = jnp.exp(alast - acum)
    e_chunk = jnp.exp(alast)

    for g in range(N_SSM_GROUPS):
        bg = bmat[:, g * gw:(g + 1) * gw]
        cg = cmat[:, g * gw:(g + 1) * gw]
        cb = lax.dot_general(cg, bg, (((1,), (1,)), ((), ())), preferred_element_type=F32)
        for r in range(SSM_R):
            h = g * SSM_R + r
            lo, hi = h * SSM_HEAD_DIM, (h + 1) * SSM_HEAD_DIM
            xh = xs[:, lo:hi]
            xdt = xh * dt[:, h:h + 1]
            seg = acum[:, h:h + 1] - acum_t[h:h + 1, :]
            decay = jnp.exp(jnp.where(causal, seg, -jnp.inf))
            y = jnp.dot((cb * decay).astype(BF16), xdt.astype(BF16), preferred_element_type=F32)
            hold = hn_ref[h]
            yoff = lax.dot_general(cg, hold.astype(BF16), (((1,), (1,)), ((), ())), preferred_element_type=F32)
            y = y + yoff * e_acum[:, h:h + 1] + dskip_ref[h] * xh
            ybuf[:, lo:hi] = y
            xe = (xdt * e_end[:, h:h + 1]).astype(BF16)
            st = lax.dot_general(xe, bg, (((0,), (0,)), ((), ())), preferred_element_type=F32)
            hn_ref[h] = hold * e_chunk[:, h:h + 1] + st

    z = blk[:, OFF_Z:OFF_XBC]
    gated = ybuf[0:qin, :] * _silu(z)
    gsz = D_SSM // N_SSM_GROUPS
    nw = nw_ref[...]
    for g in range(N_SSM_GROUPS):
        gg = gated[:, g * gsz:(g + 1) * gsz]
        gg = gg * lax.rsqrt(jnp.mean(gg * gg, axis=-1, keepdims=True) + EPS)
        o_ref[:, g * gsz:(g + 1) * gsz] = (gg * nw[:, g * gsz:(g + 1) * gsz]).astype(o_ref.dtype)


def ssd_mixer(proj, conv_init, h0, conv_w, conv_b, dt_bias, a_log, d_skip, norm_w, qin, valid):
    b, t, _ = proj.shape
    nc = t // qin
    vec = lambda a: a.reshape(1, -1)
    const2 = lambda shape: pl.BlockSpec(shape, lambda i, c: (0, 0))
    state_spec = pl.BlockSpec((None, N_SSM_HEADS, SSM_HEAD_DIM, D_STATE), lambda i, c: (i, 0, 0, 0))
    hist_spec = pl.BlockSpec((None, SUBLANES, CONV_DIM), lambda i, c: (i, 0, 0))
    return pl.pallas_call(
        functools.partial(_ssd_kernel, valid),
        grid=(b, nc),
        in_specs=[pl.BlockSpec(memory_space=pltpu.SMEM),
                  pl.BlockSpec((None, qin, P_IN), lambda i, c: (i, c, 0)),
                  hist_spec, state_spec,
                  const2((CONV_W, CONV_DIM)), const2((1, CONV_DIM)), const2((1, N_SSM_HEADS)),
                  const2((1, N_SSM_HEADS)), const2((1, D_SSM))],
        out_specs=[pl.BlockSpec((None, qin, D_SSM), lambda i, c: (i, c, 0)), hist_spec, state_spec],
        out_shape=[jax.ShapeDtypeStruct((b, t, D_SSM), BF16),
                   jax.ShapeDtypeStruct((b, SUBLANES, CONV_DIM), F32),
                   jax.ShapeDtypeStruct(h0.shape, F32)],
        scratch_shapes=[pltpu.VMEM((SUBLANES + CHUNK, CONV_DIM), F32), pltpu.VMEM((CHUNK, D_SSM), F32)],
        compiler_params=_cparams(2),
        name="ssd_mixer",
    )(d_skip, proj, conv_init, h0, conv_w, vec(conv_b), vec(dt_bias), vec(a_log), vec(norm_w))


def _layer(x, mod, lw, wdown_bf, attn_fn, conv_init, h0, seq_shape, qin, valid):
    (norm1_w, w_in, conv_w, conv_b, dt_bias, a_log, d_skip, ssm_norm_w, w_out, norm2_w, w_gate, w_up) = lw
    sh1, sc1, g1, sh2, sc2, g2 = mod
    h = norm_mod(x, norm1_w, sc1, sh1)
    proj = mm_inproj(h, w_in).reshape(*seq_shape, P_IN)
    o_attn, k_keep, v_keep = attn_fn(proj)
    o_ssm, tail, h_new = ssd_mixer(proj, conv_init, h0, conv_w, conv_b, dt_bias, a_log, d_skip, ssm_norm_w,
                                   qin, valid)
    o_attn = o_attn.reshape(*x.shape[:2], Q_W)
    o_ssm = o_ssm.reshape(*x.shape[:2], D_SSM)
    x = mm_outproj(o_attn, o_ssm, w_out, x, g1)
    h2 = norm_mod(x, norm2_w, sc2, sh2)
    hid = mm_gateup(h2, w_gate, w_up)
    x = mm_down(hid, wdown_bf, x, g2)
    return x, k_keep, v_keep, tail[:, SUBLANES - (CONV_W - 1):], h_new


def kernel(x_prompt, x_sample, cache_k, cache_v, state_conv, state_ssm, c_prompt, c_sample, w_ada, b_ada,
           norm1_w, w_in, conv_w, conv_b, dt_bias, a_log, d_skip, ssm_norm_w, sinks, w_out, norm2_w,
           w_gate, w_up, w_down, final_norm_w):
    depth = w_ada.shape[0]
    bp, tp, d = x_prompt.shape
    bs, ts, _ = x_sample.shape
    wc = cache_k.shape[2]
    assert ts <= SAMPLE_ROWS and tp % CHUNK == 0 and ts % CHUNK != 0

    c_all = jnp.concatenate([c_prompt, c_sample, jnp.zeros((ADA_ROWS - bp - bs, d), F32)], axis=0)
    mod_all = ada_mod(c_all, w_ada, b_ada)
    wdown_bf = cast_bf16(w_down)

    tab_p = rope_tables(jnp.arange(tp, dtype=jnp.int32))
    tab_s = rope_tables(PAST_LEN + jnp.arange(SAMPLE_ROWS, dtype=jnp.int32))

    hp = x_prompt
    hs = jnp.pad(x_sample, ((0, 0), (0, SAMPLE_ROWS - ts), (0, 0))).reshape(1, bs * SAMPLE_ROWS, d)
    conv0_p = jnp.zeros((bp, SUBLANES, CONV_DIM), F32)
    h0_p = jnp.zeros((bp, N_SSM_HEADS, SSM_HEAD_DIM, D_STATE), F32)

    outs = [[] for _ in range(8)]
    for l in range(depth):
        lw = (norm1_w[l], w_in[l], conv_w[l], conv_b[l], dt_bias[l], a_log[l], d_skip[l], ssm_norm_w[l],
              w_out[l], norm2_w[l], w_gate[l], w_up[l])
        mod_p = [mod_all[l, :bp, i * d:(i + 1) * d][:, None, :] for i in range(6)]
        mod_s = [jnp.repeat(mod_all[l, bp:bp + bs, i * d:(i + 1) * d], SAMPLE_ROWS, axis=0)[None] for i in range(6)]

        attn_p = functools.partial(swa_prompt, sinks=sinks[l], tables=tab_p)
        hp, k_, v_, cv_, s_ = _layer(hp, mod_p, lw, wdown_bf[l], attn_p, conv0_p, h0_p, (bp, tp), CHUNK, CHUNK)
        for lst, val in zip(outs[:4], (k_.reshape(bp, -1, N_KV, HEAD_DIM), v_.reshape(bp, -1, N_KV, HEAD_DIM), cv_, s_)):
            lst.append(val)

        attn_s = functools.partial(swa_decode, sinks=sinks[l], cache_k=cache_k[l].reshape(bs, wc, KV_W),
                                   cache_v=cache_v[l].reshape(bs, wc, KV_W), tables=tab_s, t_new=ts)
        conv_init_s = jnp.pad(state_conv[l], ((0, 0), (SUBLANES - (CONV_W - 1), 0), (0, 0)))
        hs, k_, v_, cv_, s_ = _layer(hs, mod_s, lw, wdown_bf[l], attn_s, conv_init_s, state_ssm[l],
                                     (bs, SAMPLE_ROWS), SAMPLE_ROWS, ts)
        for lst, val in zip(outs[4:], (k_.reshape(bs, wc, N_KV, HEAD_DIM), v_.reshape(bs, wc, N_KV, HEAD_DIM), cv_, s_)):
            lst.append(val)

    y_prompt = final_norm(hp, final_norm_w)
    y_sample = final_norm(hs, final_norm_w).reshape(bs, SAMPLE_ROWS, d)[:, :ts]
    return (y_prompt, y_sample) + tuple(jnp.stack(o) for o in outs)
```

```python
import functools

import jax
import jax.numpy as jnp
from jax import lax
from jax.experimental import pallas as pl
from jax.experimental.pallas import tpu as pltpu

F32 = jnp.float32
BF16 = jnp.bfloat16

D_MODEL = 4096
HEAD_DIM = 64
N_HEADS = 32
N_KV = 4
Q_PER_KV = N_HEADS // N_KV
WINDOW = 128
ROT_DIM = 16
ROPE_THETA = 500000.0
PAST_LEN = 16384
D_SSM = 2048
N_SSM_HEADS = 32
SSM_HEAD_DIM = 64
N_SSM_GROUPS = 4
SSM_R = N_SSM_HEADS // N_SSM_GROUPS
D_STATE = 128
CONV_W = 4
CONV_DIM = D_SSM + 2 * N_SSM_GROUPS * D_STATE
CHUNK = 128
D_FF = 11008
EPS = 1e-6
Q_W = N_HEADS * HEAD_DIM
KV_W = N_KV * HEAD_DIM
OFF_K = Q_W
OFF_V = Q_W + KV_W
OFF_Z = Q_W + 2 * KV_W
OFF_XBC = OFF_Z + D_SSM
OFF_DT = OFF_XBC + CONV_DIM
P_IN = OFF_DT + N_SSM_HEADS

LANES = 128
SUBLANES = 8
VMEM_LIMIT_BYTES = 56 * 1024 * 1024

P_PAD = -(-P_IN // LANES) * LANES
SAMPLE_ROWS = 8
ADA_ROWS = 40
SPLIT_PARTS = 3
STATE_ROWS = N_SSM_HEADS * SSM_HEAD_DIM


def _cparams(n_axes):
    return pltpu.CompilerParams(dimension_semantics=("arbitrary",) * n_axes, vmem_limit_bytes=VMEM_LIMIT_BYTES)


def _silu(x):
    return x * jax.nn.sigmoid(x)


def _dot(a, b):
    return jnp.dot(a, b, preferred_element_type=F32)


def _dot_nt(a, b):
    return lax.dot_general(a, b, (((1,), (1,)), ((), ())), preferred_element_type=F32)


def _dot_tn(a, b):
    return lax.dot_general(a, b, (((0,), (0,)), ((), ())), preferred_element_type=F32)


def _ada_kernel(c_ref, w_ref, b_ref, o_ref):
    x = _silu(c_ref[...]).astype(BF16)
    w = w_ref[...].astype(BF16)
    o_ref[...] = _dot(x, w) + b_ref[...]


def ada_mod(c_all, w_ada, b_ada, tn=1024):
    depth, d, n = w_ada.shape
    rows = c_all.shape[0]
    return pl.pallas_call(
        _ada_kernel,
        grid=(depth, n // tn),
        in_specs=[
            pl.BlockSpec((rows, d), lambda l, j: (0, 0)),
            pl.BlockSpec((None, d, tn), lambda l, j: (l, 0, j)),
            pl.BlockSpec((None, 1, tn), lambda l, j: (l, 0, j)),
        ],
        out_specs=pl.BlockSpec((None, rows, tn), lambda l, j: (l, 0, j)),
        out_shape=jax.ShapeDtypeStruct((depth, rows, n), F32),
        compiler_params=_cparams(2),
        name="ada_mod",
    )(c_all, w_ada, b_ada.reshape(depth, 1, n))


def _norm_mod_kernel(x_ref, nw_ref, sc_ref, sh_ref, o_ref):
    x = x_ref[...]
    y = x * lax.rsqrt(jnp.mean(x * x, axis=-1, keepdims=True) + EPS) * nw_ref[...]
    o_ref[...] = (y * (1.0 + sc_ref[...]) + sh_ref[...]).astype(o_ref.dtype)


def _norm_kernel(x_ref, nw_ref, o_ref):
    x = x_ref[...]
    o_ref[...] = x * lax.rsqrt(jnp.mean(x * x, axis=-1, keepdims=True) + EPS) * nw_ref[...]


def _row_tile(t, cap):
    tt = min(t, cap)
    assert t % tt == 0
    return tt


def norm_mod(x, nw, sc, sh, tt_cap=256):
    b, t, d = x.shape
    tt = _row_tile(t, tt_cap)
    mt = sc.shape[1]
    mod_spec = (pl.BlockSpec((None, 1, d), lambda i, j: (i, 0, 0)) if mt == 1
                else pl.BlockSpec((None, tt, d), lambda i, j: (i, j, 0)))
    return pl.pallas_call(
        _norm_mod_kernel,
        grid=(b, t // tt),
        in_specs=[pl.BlockSpec((None, tt, d), lambda i, j: (i, j, 0)),
                  pl.BlockSpec((1, d), lambda i, j: (0, 0)), mod_spec, mod_spec],
        out_specs=pl.BlockSpec((None, tt, d), lambda i, j: (i, j, 0)),
        out_shape=jax.ShapeDtypeStruct((b, t, d), BF16),
        compiler_params=_cparams(2),
        name="norm_mod",
    )(x, nw.reshape(1, d), sc, sh)


def final_norm(x, nw, tt_cap=256):
    b, t, d = x.shape
    tt = _row_tile(t, tt_cap)
    return pl.pallas_call(
        _norm_kernel,
        grid=(b, t // tt),
        in_specs=[pl.BlockSpec((None, tt, d), lambda i, j: (i, j, 0)),
                  pl.BlockSpec((1, d), lambda i, j: (0, 0))],
        out_specs=pl.BlockSpec((None, tt, d), lambda i, j: (i, j, 0)),
        out_shape=jax.ShapeDtypeStruct((b, t, d), F32),
        compiler_params=_cparams(2),
        name="final_norm",
    )(x, nw.reshape(1, d))


def _first_row_tile():
    return jnp.logical_and(pl.program_id(1) == 0, pl.program_id(2) == 0)


def _inproj_kernel(h_ref, w_ref, o_ref, wbf):
    @pl.when(_first_row_tile())
    def _():
        wbf[...] = w_ref[...].astype(BF16)

    o_ref[...] = _dot(h_ref[...], wbf[...])


def mm_inproj(h, w, layer, tm_cap=1024, tn=512):
    b, t, k = h.shape
    tm = _row_tile(t, tm_cap)
    return pl.pallas_call(
        _inproj_kernel,
        grid=(pl.cdiv(P_PAD, tn), b, t // tm),
        in_specs=[pl.BlockSpec((None, tm, k), lambda j, i, m: (i, m, 0)),
                  pl.BlockSpec((None, k, tn), lambda j, i, m: (layer, 0, j))],
        out_specs=pl.BlockSpec((None, tm, tn), lambda j, i, m: (i, m, j)),
        out_shape=jax.ShapeDtypeStruct((b, t, P_PAD), F32),
        scratch_shapes=[pltpu.VMEM((k, tn), BF16)],
        compiler_params=_cparams(3),
        name="mm_inproj",
    )(h, w)


def _outproj_kernel(oa_ref, os_ref, w_ref, x_ref, g_ref, o_ref, wbf):
    @pl.when(_first_row_tile())
    def _():
        wbf[...] = w_ref[...].astype(BF16)

    ka = oa_ref.shape[-1]
    acc = _dot(oa_ref[...], wbf[:ka, :]) + _dot(os_ref[...], wbf[ka:, :])
    o_ref[...] = x_ref[...] + g_ref[...] * acc


def _gate_spec(g, tm, tn):
    if g.shape[1] == 1:
        return pl.BlockSpec((None, 1, tn), lambda j, i, m: (i, 0, j))
    return pl.BlockSpec((None, tm, tn), lambda j, i, m: (i, m, j))


def mm_outproj(oa, os_, w, layer, x, g, tm_cap=1024, tn=512):
    b, t, ka = oa.shape
    _, k, n = w.shape
    tm = _row_tile(t, tm_cap)
    return pl.pallas_call(
        _outproj_kernel,
        grid=(n // tn, b, t // tm),
        in_specs=[pl.BlockSpec((None, tm, ka), lambda j, i, m: (i, m, 0)),
                  pl.BlockSpec((None, tm, k - ka), lambda j, i, m: (i, m, 0)),
                  pl.BlockSpec((None, k, tn), lambda j, i, m: (layer, 0, j)),
                  pl.BlockSpec((None, tm, tn), lambda j, i, m: (i, m, j)),
                  _gate_spec(g, tm, tn)],
        out_specs=pl.BlockSpec((None, tm, tn), lambda j, i, m: (i, m, j)),
        out_shape=jax.ShapeDtypeStruct((b, t, n), F32),
        scratch_shapes=[pltpu.VMEM((k, tn), BF16)],
        compiler_params=_cparams(3),
        name="mm_outproj",
    )(oa, os_, w, x, g)


def _gateup_kernel(h_ref, wg_ref, wu_ref, o_ref, wgbf, wubf):
    @pl.when(_first_row_tile())
    def _():
        wgbf[...] = wg_ref[...].astype(BF16)
        wubf[...] = wu_ref[...].astype(BF16)

    h = h_ref[...]
    o_ref[...] = (_silu(_dot(h, wgbf[...])) * _dot(h, wubf[...])).astype(o_ref.dtype)


def mm_gateup(h, wg, wu, layer, tm_cap=1024, tn=256):
    b, t, k = h.shape
    n = wg.shape[2]
    assert n % tn == 0
    tm = _row_tile(t, tm_cap)
    wspec = pl.BlockSpec((None, k, tn), lambda j, i, m: (layer, 0, j))
    return pl.pallas_call(
        _gateup_kernel,
        grid=(n // tn, b, t // tm),
        in_specs=[pl.BlockSpec((None, tm, k), lambda j, i, m: (i, m, 0)), wspec, wspec],
        out_specs=pl.BlockSpec((None, tm, tn), lambda j, i, m: (i, m, j)),
        out_shape=jax.ShapeDtypeStruct((b, t, n), BF16),
        scratch_shapes=[pltpu.VMEM((k, tn), BF16), pltpu.VMEM((k, tn), BF16)],
        compiler_params=_cparams(3),
        name="mm_gateup",
    )(h, wg, wu)


def _cast_kernel(w_ref, o_ref):
    o_ref[...] = w_ref[...].astype(o_ref.dtype)


def cast_bf16(w, tr=256):
    depth, r, c = w.shape
    assert r % tr == 0
    return pl.pallas_call(
        _cast_kernel,
        grid=(depth, r // tr),
        in_specs=[pl.BlockSpec((None, tr, c), lambda l, i: (l, i, 0))],
        out_specs=pl.BlockSpec((None, tr, c), lambda l, i: (l, i, 0)),
        out_shape=jax.ShapeDtypeStruct(w.shape, BF16),
        compiler_params=_cparams(2),
        name="cast_bf16",
    )(w)


def _down_kernel(h_ref, w_ref, x_ref, g_ref, o_ref):
    o_ref[...] = x_ref[...] + g_ref[...] * _dot(h_ref[...], w_ref[...])


def mm_down(hid, wbf, layer, x, g, tm_cap=512, tn=512):
    b, t, k = hid.shape
    n = wbf.shape[2]
    tm = _row_tile(t, tm_cap)
    gspec = (pl.BlockSpec((None, 1, tn), lambda i, m, j: (i, 0, j)) if g.shape[1] == 1
             else pl.BlockSpec((None, tm, tn), lambda i, m, j: (i, m, j)))
    return pl.pallas_call(
        _down_kernel,
        grid=(b, t // tm, n // tn),
        in_specs=[pl.BlockSpec((None, tm, k), lambda i, m, j: (i, m, 0)),
                  pl.BlockSpec((None, k, tn), lambda i, m, j: (layer, 0, j)),
                  pl.BlockSpec((None, tm, tn), lambda i, m, j: (i, m, j)),
                  gspec],
        out_specs=pl.BlockSpec((None, tm, tn), lambda i, m, j: (i, m, j)),
        out_shape=jax.ShapeDtypeStruct((b, t, n), F32),
        compiler_params=_cparams(3),
        name="mm_down",
    )(hid, wbf, x, g)


def rope_tables(pos):
    half = ROT_DIM // 2
    inv = ROPE_THETA ** (-jnp.arange(0, ROT_DIM, 2, dtype=F32) / ROT_DIM)
    ang = pos.astype(F32)[:, None] * inv[None, :]
    cos, sin = jnp.cos(ang), jnp.sin(ang)
    t = pos.shape[0]
    rest = HEAD_DIM - ROT_DIM
    c = jnp.concatenate([cos, cos, jnp.ones((t, rest), F32)], axis=1)
    s_up = jnp.concatenate([-sin, jnp.zeros((t, half + rest), F32)], axis=1)
    s_dn = jnp.concatenate([jnp.zeros((t, half), F32), sin, jnp.zeros((t, rest), F32)], axis=1)
    reps = LANES // HEAD_DIM
    return tuple(jnp.tile(a, (1, reps)) for a in (c, s_up, s_dn))


def _rope(x, c, s_up, s_dn):
    w = x.shape[-1]
    reps = w // LANES
    half = ROT_DIM // 2
    tile = lambda a: jnp.concatenate([a] * reps, axis=1) if reps > 1 else a
    return (x * tile(c) + pltpu.roll(x, w - half, 1) * tile(s_up) + pltpu.roll(x, half, 1) * tile(s_dn))


def _low_head(shape):
    return lax.broadcasted_iota(jnp.int32, shape, len(shape) - 1) % LANES < HEAD_DIM


def _swa_prompt_kernel(sinks_ref, q_ref, kv_ref, c_ref, su_ref, sd_ref, o_ref, nk_ref, nv_ref, kprev, vprev):
    n = pl.program_id(1)
    last = pl.num_programs(1) - 1
    blk = q_ref.shape[0]

    @pl.when(n == 0)
    def _():
        kprev[...] = jnp.zeros_like(kprev)
        vprev[...] = jnp.zeros_like(vprev)

    c, su, sd = c_ref[...], su_ref[...], sd_ref[...]
    q = _rope(q_ref[...], c, su, sd) * (HEAD_DIM ** -0.5)
    low = _low_head(q.shape)
    q_heads = (jnp.where(low, q, 0.0).astype(BF16), jnp.where(low, 0.0, q).astype(BF16))
    kv = kv_ref[...]
    k = _rope(kv[:, :KV_W], c, su, sd)
    v = kv[:, KV_W:]

    @pl.when(n == last)
    def _():
        nk_ref[...] = k
        nv_ref[...] = v

    kk = jnp.concatenate([kprev[(n + 1) % 2], k], axis=0)
    vv = jnp.concatenate([vprev[(n + 1) % 2], v], axis=0)
    low2 = _low_head((2 * blk, LANES))
    low1 = _low_head((blk, LANES))
    row = lax.broadcasted_iota(jnp.int32, (blk, blk), 0)
    col = lax.broadcasted_iota(jnp.int32, (blk, blk), 1)
    from_prev = col > row
    no_prev = jnp.where(n > 0, 0.0, -jnp.inf)

    def both_halves(x, g):
        tile = x[:, (g // 2) * LANES:(g // 2 + 1) * LANES]
        other = pltpu.roll(tile, HEAD_DIM, 1)
        return (jnp.where(low2, other, tile) if g % 2 else jnp.where(low2, tile, other)).astype(BF16)

    for g in range(N_KV):
        kd, vd = both_halves(kk, g), both_halves(vv, g)
        for jp in range(Q_PER_KV // 2):
            lanes = slice((g * Q_PER_KV // 2 + jp) * LANES, (g * Q_PER_KV // 2 + jp + 1) * LANES)
            outs = []
            for half in range(2):
                h = g * Q_PER_KV + 2 * jp + half
                s = _dot_nt(q_heads[half][:, lanes], kd)
                s = jnp.where(from_prev, s[:, :blk] + no_prev, s[:, blk:])
                sink = sinks_ref[h]
                m = jnp.maximum(jnp.max(s, axis=-1, keepdims=True), sink)
                e = jnp.exp(s - m)
                den = jnp.sum(e, axis=-1, keepdims=True) + jnp.exp(sink - m)
                pe = jnp.concatenate([jnp.where(from_prev, e, 0.0), jnp.where(from_prev, 0.0, e)], axis=1)
                outs.append(_dot(pe.astype(BF16), vd) * (1.0 / den))
            o_ref[:, lanes] = jnp.where(low1, outs[0], outs[1]).astype(o_ref.dtype)
    kprev[n % 2] = k
    vprev[n % 2] = v


def swa_prompt(proj, sinks, tables):
    b, t, _ = proj.shape
    blk = WINDOW
    nb = t // blk
    kv_col = OFF_K // (2 * KV_W)
    tab_spec = pl.BlockSpec((blk, LANES), lambda i, n: (n, 0))
    keep_spec = pl.BlockSpec((None, blk, KV_W), lambda i, n: (i, 0, 0))
    return pl.pallas_call(
        _swa_prompt_kernel,
        grid=(b, nb),
        in_specs=[pl.BlockSpec(memory_space=pltpu.SMEM),
                  pl.BlockSpec((None, blk, Q_W), lambda i, n: (i, n, 0)),
                  pl.BlockSpec((None, blk, 2 * KV_W), lambda i, n: (i, n, kv_col)),
                  tab_spec, tab_spec, tab_spec],
        out_specs=[pl.BlockSpec((None, blk, Q_W), lambda i, n: (i, n, 0)), keep_spec, keep_spec],
        out_shape=[jax.ShapeDtypeStruct((b, t, Q_W), BF16),
                   jax.ShapeDtypeStruct((b, blk, KV_W), F32),
                   jax.ShapeDtypeStruct((b, blk, KV_W), F32)],
        scratch_shapes=[pltpu.VMEM((2, blk, KV_W), F32), pltpu.VMEM((2, blk, KV_W), F32)],
        compiler_params=_cparams(2),
        name="swa_prompt",
    )(sinks, proj, proj, *tables)


def _swa_decode_kernel(t_new, sinks_ref, q_ref, kv_ref, kc_ref, vc_ref, c_ref, su_ref, sd_ref,
                       o_ref, nk_ref, nv_ref, kk, vv):
    rows = q_ref.shape[0]
    wc = kc_ref.shape[0]
    c, su, sd = c_ref[...], su_ref[...], sd_ref[...]
    q = (_rope(q_ref[...], c, su, sd) * (HEAD_DIM ** -0.5)).astype(BF16)
    kv = kv_ref[...]
    k = _rope(kv[:, :KV_W], c, su, sd)
    v = kv[:, KV_W:]
    kc, vc = kc_ref[...], vc_ref[...]

    nk_ref[0:wc - t_new, :] = kc[t_new:, :]
    nv_ref[0:wc - t_new, :] = vc[t_new:, :]
    nk_ref[wc - t_new:wc, :] = k[0:t_new, :]
    nv_ref[wc - t_new:wc, :] = v[0:t_new, :]

    kk[...] = jnp.zeros_like(kk)
    vv[...] = jnp.zeros_like(vv)
    kk[0:wc, :] = kc.astype(BF16)
    vv[0:wc, :] = vc.astype(BF16)
    kk[wc:wc + rows, :] = k.astype(BF16)
    vv[wc:wc + rows, :] = v.astype(BF16)

    nkeys = kk.shape[0]
    nrow = Q_PER_KV * rows
    r = lax.broadcasted_iota(jnp.int32, (nrow, nkeys), 0)
    sj = lax.broadcasted_iota(jnp.int32, (nrow, nkeys), 1)
    tq = r % rows
    mask = jnp.logical_and(jnp.logical_and(sj <= tq + wc, sj > tq + wc - WINDOW), sj < wc + t_new)
    rcol = lax.broadcasted_iota(jnp.int32, (nrow, 1), 0) // rows
    for g in range(N_KV):
        kg = kk[:, g * HEAD_DIM:(g + 1) * HEAD_DIM]
        vg = vv[:, g * HEAD_DIM:(g + 1) * HEAD_DIM]
        qs = jnp.concatenate([q[:, (g * Q_PER_KV + hh) * HEAD_DIM:(g * Q_PER_KV + hh + 1) * HEAD_DIM]
                              for hh in range(Q_PER_KV)], axis=0)
        sink = jnp.zeros((nrow, 1), F32)
        for hh in range(Q_PER_KV):
            sink = jnp.where(rcol == hh, sinks_ref[g * Q_PER_KV + hh], sink)
        s = jnp.where(mask, _dot_nt(qs, kg), -jnp.inf)
        m = jnp.maximum(jnp.max(s, axis=-1, keepdims=True), sink)
        e = jnp.exp(s - m)
        p = e / (jnp.sum(e, axis=-1, keepdims=True) + jnp.exp(sink - m))
        og = _dot(p.astype(BF16), vg)
        for hh in range(Q_PER_KV):
            h = g * Q_PER_KV + hh
            o_ref[:, h * HEAD_DIM:(h + 1) * HEAD_DIM] = og[hh * rows:(hh + 1) * rows, :].astype(o_ref.dtype)


def swa_decode(proj, sinks, cache_k, cache_v, layer, tables, t_new):
    b, rows, _ = proj.shape
    wc = cache_k.shape[2]
    kv_col = OFF_K // (2 * KV_W)
    tab_spec = pl.BlockSpec((rows, LANES), lambda i: (0, 0))
    cache_in = pl.BlockSpec((None, None, wc, KV_W), lambda i: (layer, i, 0, 0))
    cache_out = pl.BlockSpec((None, wc, KV_W), lambda i: (i, 0, 0))
    nkeys = 2 * WINDOW
    return pl.pallas_call(
        functools.partial(_swa_decode_kernel, t_new),
        grid=(b,),
        in_specs=[pl.BlockSpec(memory_space=pltpu.SMEM),
                  pl.BlockSpec((None, rows, Q_W), lambda i: (i, 0, 0)),
                  pl.BlockSpec((None, rows, 2 * KV_W), lambda i: (i, 0, kv_col)),
                  cache_in, cache_in, tab_spec, tab_spec, tab_spec],
        out_specs=[pl.BlockSpec((None, rows, Q_W), lambda i: (i, 0, 0)), cache_out, cache_out],
        out_shape=[jax.ShapeDtypeStruct((b, rows, Q_W), BF16),
                   jax.ShapeDtypeStruct((b, wc, KV_W), F32),
                   jax.ShapeDtypeStruct((b, wc, KV_W), F32)],
        scratch_shapes=[pltpu.VMEM((nkeys, KV_W), BF16), pltpu.VMEM((nkeys, KV_W), BF16)],
        compiler_params=_cparams(1),
        name="swa_decode",
    )(sinks, proj, proj, cache_k, cache_v, *tables)


def spread_matrices():
    src = jnp.arange(LANES)
    head = src % N_SSM_HEADS
    live = src < SPLIT_PARTS * N_SSM_HEADS
    to_x = (head[:, None] == (jnp.arange(D_SSM) // SSM_HEAD_DIM)[None, :]) & live[:, None]
    to_seg = (head[:, None] == (jnp.arange(N_SSM_HEADS * CHUNK) // CHUNK)[None, :]) & live[:, None]
    return to_x.astype(BF16), to_seg.astype(BF16)


def _three_copies(x):
    return x + pltpu.roll(x, N_SSM_HEADS, 1) + pltpu.roll(x, 2 * N_SSM_HEADS, 1)


def _split_parts(x3, part):
    hi = x3.astype(BF16).astype(F32)
    r1 = x3 - hi
    mid = r1.astype(BF16).astype(F32)
    lo = r1 - mid
    return jnp.where(part == 0, hi, jnp.where(part == 1, mid, jnp.where(part == 2, lo, 0.0))).astype(BF16)


def _ssd_kernel(valid, proj_ref, cinit_ref, h0_ref, cw_ref, cb_ref, dtb_ref, alog_ref, dsk_ref, nw_ref,
                tox_ref, toseg_ref, o_ref, tail_ref, hn_ref, xp, ybuf):
    cidx = pl.program_id(1)
    qin = proj_ref.shape[0]
    q = ybuf.shape[0]
    pad = q - qin
    hist = SUBLANES

    @pl.when(cidx == 0)
    def _():
        xp[0:hist, :] = cinit_ref[...]
        hn_ref[...] = h0_ref[...]

    blk = proj_ref[...]
    xp[hist:hist + qin, :] = blk[:, OFF_XBC:OFF_DT]
    if pad:
        xp[hist + qin:, :] = jnp.zeros((pad, CONV_DIM), F32)
    tail_ref[...] = xp[valid:valid + hist, :]

    cw = cw_ref[...]
    conv = cb_ref[...] + cw[CONV_W - 1:CONV_W, :] * xp[hist:hist + q, :]
    for j in range(CONV_W - 1):
        shift = CONV_W - 1 - j
        conv = conv + cw[j:j + 1, :] * xp[hist - shift:hist - shift + q, :]
    xc = _silu(conv)
    xp[0:hist, :] = xp[q:q + hist, :]

    xs = xc[:, :D_SSM]
    gw = D_STATE
    bmat = xc[:, D_SSM:D_SSM + N_SSM_GROUPS * gw].astype(BF16)
    cmat = xc[:, D_SSM + N_SSM_GROUPS * gw:].astype(BF16)

    lane = lax.broadcasted_iota(jnp.int32, (q, LANES), 1)
    rowi = lax.broadcasted_iota(jnp.int32, (q, LANES), 0)
    part = lane // N_SSM_HEADS
    dt_raw = blk[:, OFF_DT:OFF_DT + LANES]
    if pad:
        dt_raw = jnp.concatenate([dt_raw, jnp.zeros((pad, LANES), F32)], axis=0)
    dt_raw = _three_copies(jnp.where(lane < N_SSM_HEADS, dt_raw, 0.0))
    xdtb = dt_raw + dtb_ref[...]
    dt = jnp.maximum(xdtb, 0.0) + jnp.log1p(jnp.exp(-jnp.abs(xdtb)))
    dt = jnp.where(jnp.logical_and(rowi < valid, part < SPLIT_PARTS), dt, 0.0)
    a = dt * (-jnp.exp(alog_ref[...]))

    li = lax.broadcasted_iota(jnp.int32, (q, q), 0)
    si = lax.broadcasted_iota(jnp.int32, (q, q), 1)
    causal = li >= si
    tri = causal.astype(BF16)
    triu = (li <= si).astype(BF16)
    pa = _split_parts(a, part)
    c3 = _dot(tri, pa)
    acum = c3
    for k in range(1, LANES // N_SSM_HEADS):
        acum = acum + pltpu.roll(c3, k * N_SSM_HEADS, 1)
    ct = _dot_tn(pa, triu)
    acum_t = ct[0:N_SSM_HEADS] + ct[N_SSM_HEADS:2 * N_SSM_HEADS] + ct[2 * N_SSM_HEADS:3 * N_SSM_HEADS]
    alast = acum[q - 1:q, :]
    e_acum = jnp.exp(acum)
    e_chunk = e_acum[q - 1:q, :]

    tox = tox_ref[...]
    dtx = _dot(_split_parts(dt, part), tox)
    e_end_x = _dot(_split_parts(jnp.exp(alast - acum), part), tox)
    e_acum_x = _dot(_split_parts(e_acum, part), tox)
    pacum = _split_parts(acum, part)

    xdt = xs * dtx
    low = _low_head(xdt.shape)
    xdt_heads = (jnp.where(low, xdt, 0.0).astype(BF16), jnp.where(low, 0.0, xdt).astype(BF16))
    xe = (xdt * e_end_x).astype(BF16)
    ybuf[...] = dsk_ref[...] * xs

    gsz = SSM_R * SSM_HEAD_DIM
    for g in range(N_SSM_GROUPS):
        bg = bmat[:, g * gw:(g + 1) * gw]
        cg = cmat[:, g * gw:(g + 1) * gw]
        glanes = slice(g * gsz, (g + 1) * gsz)
        grows = slice(g * gsz, (g + 1) * gsz)
        cb = _dot_nt(cg, bg)
        hold = hn_ref[grows, :]
        ybuf[:, glanes] += _dot_nt(cg, hold.astype(BF16)) * e_acum_x[:, glanes]
        seg = _dot(pacum, toseg_ref[:, g * SSM_R * q:(g + 1) * SSM_R * q])
        for jp in range(SSM_R // 2):
            ms = []
            for half in range(2):
                r = 2 * jp + half
                h = g * SSM_R + r
                d = seg[:, r * q:(r + 1) * q] - acum_t[h:h + 1, :]
                ms.append((cb * jnp.exp(jnp.where(causal, d, -jnp.inf))).astype(BF16))
            plane = slice(g * gsz + jp * LANES, g * gsz + (jp + 1) * LANES)
            rhs = jnp.concatenate([xdt_heads[0][:, plane], xdt_heads[1][:, plane]], axis=0)
            ybuf[:, plane] += _dot(jnp.concatenate(ms, axis=1), rhs)
        st = _dot_tn(xe[:, glanes], bg)
        for r in range(SSM_R):
            h = g * SSM_R + r
            rows = slice(h * SSM_HEAD_DIM, (h + 1) * SSM_HEAD_DIM)
            hn_ref[rows, :] = (hold[r * SSM_HEAD_DIM:(r + 1) * SSM_HEAD_DIM, :] * e_chunk[:, h:h + 1]
                               + st[r * SSM_HEAD_DIM:(r + 1) * SSM_HEAD_DIM, :])

    z = blk[:, OFF_Z:OFF_XBC]
    gated = ybuf[0:qin, :] * _silu(z)
    nsz = D_SSM // N_SSM_GROUPS
    nw = nw_ref[...]
    for g in range(N_SSM_GROUPS):
        gg = gated[:, g * nsz:(g + 1) * nsz]
        gg = gg * lax.rsqrt(jnp.mean(gg * gg, axis=-1, keepdims=True) + EPS)
        o_ref[:, g * nsz:(g + 1) * nsz] = (gg * nw[:, g * nsz:(g + 1) * nsz]).astype(o_ref.dtype)


def ssd_mixer(proj, conv_init, h0, layer, conv_w, conv_b, dt_bias, a_log, d_skip, norm_w, spread, qin, valid):
    b, t, _ = proj.shape
    nc = t // qin
    copies = lambda a: jnp.concatenate([a] * SPLIT_PARTS + [jnp.zeros_like(a)]).reshape(1, LANES)
    const2 = lambda shape: pl.BlockSpec(shape, lambda i, c: (0, 0))
    state_in = pl.BlockSpec((None, None, STATE_ROWS, D_STATE), lambda i, c: (layer, i, 0, 0))
    state_out = pl.BlockSpec((None, STATE_ROWS, D_STATE), lambda i, c: (i, 0, 0))
    hist_spec = pl.BlockSpec((None, SUBLANES, CONV_DIM), lambda i, c: (i, 0, 0))
    to_x, to_seg = spread
    return pl.pallas_call(
        functools.partial(_ssd_kernel, valid),
        grid=(b, nc),
        in_specs=[pl.BlockSpec((None, qin, P_PAD), lambda i, c: (i, c, 0)),
                  hist_spec, state_in,
                  const2((CONV_W, CONV_DIM)), const2((1, CONV_DIM)), const2((1, LANES)),
                  const2((1, LANES)), const2((1, D_SSM)), const2((1, D_SSM)),
                  const2(to_x.shape), const2(to_seg.shape)],
        out_specs=[pl.BlockSpec((None, qin, D_SSM), lambda i, c: (i, c, 0)), hist_spec, state_out],
        out_shape=[jax.ShapeDtypeStruct((b, t, D_SSM), BF16),
                   jax.ShapeDtypeStruct((b, SUBLANES, CONV_DIM), F32),
                   jax.ShapeDtypeStruct((b, STATE_ROWS, D_STATE), F32)],
        scratch_shapes=[pltpu.VMEM((SUBLANES + CHUNK, CONV_DIM), F32), pltpu.VMEM((CHUNK, D_SSM), F32)],
        compiler_params=_cparams(2),
        name="ssd_mixer",
    )(proj, conv_init, h0, conv_w, conv_b.reshape(1, -1), copies(dt_bias), copies(a_log),
      jnp.repeat(d_skip, SSM_HEAD_DIM).reshape(1, -1), norm_w.reshape(1, -1), to_x, to_seg)


def _layer(x, mod, weights, layer, wdown_bf, attn_fn, conv_init, h0, h0_layer, spread, seq_shape, qin, valid):
    (norm1_w, w_in, conv_w, conv_b, dt_bias, a_log, d_skip, ssm_norm_w, w_out, norm2_w, w_gate, w_up) = weights
    sh1, sc1, g1, sh2, sc2, g2 = mod
    h = norm_mod(x, norm1_w[layer], sc1, sh1)
    proj = mm_inproj(h, w_in, layer).reshape(*seq_shape, P_PAD)
    o_attn, k_keep, v_keep = attn_fn(proj)
    o_ssm, tail, h_new = ssd_mixer(proj, conv_init, h0, h0_layer, conv_w[layer], conv_b[layer], dt_bias[layer],
                                   a_log[layer], d_skip[layer], ssm_norm_w[layer], spread, qin, valid)
    o_attn = o_attn.reshape(*x.shape[:2], Q_W)
    o_ssm = o_ssm.reshape(*x.shape[:2], D_SSM)
    x = mm_outproj(o_attn, o_ssm, w_out, layer, x, g1)
    h2 = norm_mod(x, norm2_w[layer], sc2, sh2)
    hid = mm_gateup(h2, w_gate, w_up, layer)
    x = mm_down(hid, wdown_bf, layer, x, g2)
    h_new = h_new.reshape(-1, N_SSM_HEADS, SSM_HEAD_DIM, D_STATE)
    return x, k_keep, v_keep, tail[:, SUBLANES - (CONV_W - 1):], h_new


def kernel(x_prompt, x_sample, cache_k, cache_v, state_conv, state_ssm, c_prompt, c_sample, w_ada, b_ada,
           norm1_w, w_in, conv_w, conv_b, dt_bias, a_log, d_skip, ssm_norm_w, sinks, w_out, norm2_w,
           w_gate, w_up, w_down, final_norm_w):
    depth = w_ada.shape[0]
    bp, tp, d = x_prompt.shape
    bs, ts, _ = x_sample.shape
    wc = cache_k.shape[2]
    assert ts <= SAMPLE_ROWS and tp % CHUNK == 0 and ts % CHUNK != 0

    c_all = jnp.concatenate([c_prompt, c_sample, jnp.zeros((ADA_ROWS - bp - bs, d), F32)], axis=0)
    mod_all = ada_mod(c_all, w_ada, b_ada)
    wdown_bf = cast_bf16(w_down)

    tab_p = rope_tables(jnp.arange(tp, dtype=jnp.int32))
    tab_s = rope_tables(PAST_LEN + jnp.arange(SAMPLE_ROWS, dtype=jnp.int32))
    spread = spread_matrices()
    weights = (norm1_w, w_in, conv_w, conv_b, dt_bias, a_log, d_skip, ssm_norm_w, w_out, norm2_w, w_gate, w_up)

    hp = x_prompt
    hs = jnp.pad(x_sample, ((0, 0), (0, SAMPLE_ROWS - ts), (0, 0))).reshape(1, bs * SAMPLE_ROWS, d)
    conv0_p = jnp.zeros((bp, SUBLANES, CONV_DIM), F32)
    h0_p = jnp.zeros((1, bp, STATE_ROWS, D_STATE), F32)
    h0_s = state_ssm.reshape(depth, bs, STATE_ROWS, D_STATE)
    ck = cache_k.reshape(depth, bs, wc, KV_W)
    cv = cache_v.reshape(depth, bs, wc, KV_W)

    outs = [[] for _ in range(8)]
    for l in range(depth):
        mod_p = [mod_all[l, :bp, i * d:(i + 1) * d][:, None, :] for i in range(6)]
        mod_s = [jnp.repeat(mod_all[l, bp:bp + bs, i * d:(i + 1) * d], SAMPLE_ROWS, axis=0)[None] for i in range(6)]

        attn_p = functools.partial(swa_prompt, sinks=sinks[l], tables=tab_p)
        hp, k_, v_, cv_, s_ = _layer(hp, mod_p, weights, l, wdown_bf, attn_p, conv0_p, h0_p, 0, spread,
                                     (bp, tp), CHUNK, CHUNK)
        for lst, val in zip(outs[:4], (k_.reshape(bp, -1, N_KV, HEAD_DIM), v_.reshape(bp, -1, N_KV, HEAD_DIM), cv_, s_)):
            lst.append(val)

        attn_s = functools.partial(swa_decode, sinks=sinks[l], cache_k=ck, cache_v=cv, layer=l, tables=tab_s,
                                   t_new=ts)
        conv_init_s = jnp.pad(state_conv[l], ((0, 0), (SUBLANES - (CONV_W - 1), 0), (0, 0)))
        hs, k_, v_, cv_, s_ = _layer(hs, mod_s, weights, l, wdown_bf, attn_s, conv_init_s, h0_s, l, spread,
                                     (bs, SAMPLE_ROWS), SAMPLE_ROWS, ts)
        for lst, val in zip(outs[4:], (k_.reshape(bs, wc, N_KV, HEAD_DIM), v_.reshape(bs, wc, N_KV, HEAD_DIM), cv_, s_)):
            lst.append(val)

    y_prompt = final_norm(hp, final_norm_w)
    y_sample = final_norm(hs, final_norm_w).reshape(bs, SAMPLE_ROWS, d)[:, :ts]
    return (y_prompt, y_sample) + tuple(jnp.stack(o) for o in outs)
```

```python
import functools

import jax
import jax.numpy as jnp
from jax import lax
from jax.experimental import pallas as pl
from jax.experimental.pallas import tpu as pltpu

F32 = jnp.float32
BF16 = jnp.bfloat16

D_MODEL = 4096
HEAD_DIM = 64
N_HEADS = 32
N_KV = 4
Q_PER_KV = N_HEADS // N_KV
WINDOW = 128
ROT_DIM = 16
ROPE_THETA = 500000.0
PAST_LEN = 16384
D_SSM = 2048
N_SSM_HEADS = 32
SSM_HEAD_DIM = 64
N_SSM_GROUPS = 4
SSM_R = N_SSM_HEADS // N_SSM_GROUPS
D_STATE = 128
CONV_W = 4
CONV_DIM = D_SSM + 2 * N_SSM_GROUPS * D_STATE
CHUNK = 128
D_FF = 11008
EPS = 1e-6
Q_W = N_HEADS * HEAD_DIM
KV_W = N_KV * HEAD_DIM
OFF_K = Q_W
OFF_V = Q_W + KV_W
OFF_Z = Q_W + 2 * KV_W
OFF_XBC = OFF_Z + D_SSM
OFF_DT = OFF_XBC + CONV_DIM
P_IN = OFF_DT + N_SSM_HEADS

LANES = 128
SUBLANES = 8
VMEM_LIMIT_BYTES = 56 * 1024 * 1024

P_PAD = -(-P_IN // LANES) * LANES
SAMPLE_ROWS = 8
ADA_ROWS = 40
SPLIT_PARTS = 3
STATE_ROWS = N_SSM_HEADS * SSM_HEAD_DIM


def _cparams(n_axes):
    return pltpu.CompilerParams(dimension_semantics=("arbitrary",) * n_axes, vmem_limit_bytes=VMEM_LIMIT_BYTES)


def _silu(x):
    return x * jax.nn.sigmoid(x)


def _dot(a, b):
    return jnp.dot(a, b, preferred_element_type=F32)


def _dot_nt(a, b):
    return lax.dot_general(a, b, (((1,), (1,)), ((), ())), preferred_element_type=F32)


def _dot_tn(a, b):
    return lax.dot_general(a, b, (((0,), (0,)), ((), ())), preferred_element_type=F32)


def _ada_kernel(c_ref, w_ref, b_ref, o_ref):
    x = _silu(c_ref[...]).astype(BF16)
    w = w_ref[...].astype(BF16)
    o_ref[...] = _dot(x, w) + b_ref[...]


def ada_mod(c_all, w_ada, b_ada, tn=1024):
    depth, d, n = w_ada.shape
    rows = c_all.shape[0]
    return pl.pallas_call(
        _ada_kernel,
        grid=(depth, n // tn),
        in_specs=[
            pl.BlockSpec((rows, d), lambda l, j: (0, 0)),
            pl.BlockSpec((None, d, tn), lambda l, j: (l, 0, j)),
            pl.BlockSpec((None, 1, tn), lambda l, j: (l, 0, j)),
        ],
        out_specs=pl.BlockSpec((None, rows, tn), lambda l, j: (l, 0, j)),
        out_shape=jax.ShapeDtypeStruct((depth, rows, n), F32),
        compiler_params=_cparams(2),
        name="ada_mod",
    )(c_all, w_ada, b_ada.reshape(depth, 1, n))


def _norm_mod_kernel(x_ref, nw_ref, sc_ref, sh_ref, o_ref):
    x = x_ref[...]
    y = x * lax.rsqrt(jnp.mean(x * x, axis=-1, keepdims=True) + EPS) * nw_ref[...]
    o_ref[...] = (y * (1.0 + sc_ref[...]) + sh_ref[...]).astype(o_ref.dtype)


def _norm_kernel(x_ref, nw_ref, o_ref):
    x = x_ref[...]
    o_ref[...] = x * lax.rsqrt(jnp.mean(x * x, axis=-1, keepdims=True) + EPS) * nw_ref[...]


def _row_tile(t, cap):
    tt = min(t, cap)
    assert t % tt == 0
    return tt


def norm_mod(x, nw, sc, sh, tt_cap=256):
    b, t, d = x.shape
    tt = _row_tile(t, tt_cap)
    mt = sc.shape[1]
    mod_spec = (pl.BlockSpec((None, 1, d), lambda i, j: (i, 0, 0)) if mt == 1
                else pl.BlockSpec((None, tt, d), lambda i, j: (i, j, 0)))
    return pl.pallas_call(
        _norm_mod_kernel,
        grid=(b, t // tt),
        in_specs=[pl.BlockSpec((None, tt, d), lambda i, j: (i, j, 0)),
                  pl.BlockSpec((1, d), lambda i, j: (0, 0)), mod_spec, mod_spec],
        out_specs=pl.BlockSpec((None, tt, d), lambda i, j: (i, j, 0)),
        out_shape=jax.ShapeDtypeStruct((b, t, d), BF16),
        compiler_params=_cparams(2),
        name="norm_mod",
    )(x, nw.reshape(1, d), sc, sh)


def final_norm(x, nw, tt_cap=256):
    b, t, d = x.shape
    tt = _row_tile(t, tt_cap)
    return pl.pallas_call(
        _norm_kernel,
        grid=(b, t // tt),
        in_specs=[pl.BlockSpec((None, tt, d), lambda i, j: (i, j, 0)),
                  pl.BlockSpec((1, d), lambda i, j: (0, 0))],
        out_specs=pl.BlockSpec((None, tt, d), lambda i, j: (i, j, 0)),
        out_shape=jax.ShapeDtypeStruct((b, t, d), F32),
        compiler_params=_cparams(2),
        name="final_norm",
    )(x, nw.reshape(1, d))


def _streams(b, t, tm):
    assert t % tm == 0
    per_b = t // tm
    steps = b * per_b

    def prompt_idx(s):
        sp = jnp.minimum(s, steps - 1)
        return sp // per_b, sp % per_b

    return steps, prompt_idx


def _stream_specs(pidx, tm, ms, width, col=None):
    if col is None:
        return (pl.BlockSpec((None, tm, width), lambda j, s: (*pidx(s), 0)),
                pl.BlockSpec((ms, width), lambda j, s: (0, 0)))
    return (pl.BlockSpec((None, tm, col), lambda j, s: (*pidx(s), j)),
            pl.BlockSpec((ms, col), lambda j, s: (0, j)))


def _on_streams(prompt_fn, sample_fn):
    s = pl.program_id(1)
    sample_step = pl.num_programs(1) - 1
    pl.when(s < sample_step)(prompt_fn)
    pl.when(s == sample_step)(sample_fn)


def _inproj_kernel(hp_ref, hs_ref, wt_ref, op_ref, os_ref, wbf):
    @pl.when(pl.program_id(1) == 0)
    def _():
        wbf[...] = wt_ref[...].T.astype(BF16)

    def run(h_ref, o_ref):
        o_ref[...] = _dot(h_ref[...], wbf[...])

    _on_streams(functools.partial(run, hp_ref, op_ref), functools.partial(run, hs_ref, os_ref))


def mm_inproj(hp, hs, wt, layer, tm=1024, tn=512):
    b, t, k = hp.shape
    ms = hs.shape[0]
    tm = min(tm, t)
    steps, pidx = _streams(b, t, tm)
    return pl.pallas_call(
        _inproj_kernel,
        grid=(pl.cdiv(P_PAD, tn), steps + 1),
        in_specs=[*_stream_specs(pidx, tm, ms, k),
                  pl.BlockSpec((None, tn, k), lambda j, s: (layer, j, 0))],
        out_specs=list(_stream_specs(pidx, tm, ms, P_PAD, tn)),
        out_shape=[jax.ShapeDtypeStruct((b, t, P_PAD), F32), jax.ShapeDtypeStruct((ms, P_PAD), F32)],
        scratch_shapes=[pltpu.VMEM((k, tn), BF16)],
        compiler_params=_cparams(2),
        name="mm_inproj",
    )(hp, hs, wt)


def _outproj_kernel(ap_ref, sp_ref, as_ref, ss_ref, w_ref, xp_ref, xs_ref, gp_ref, gs_ref, op_ref, os_ref, wbf):
    @pl.when(pl.program_id(1) == 0)
    def _():
        wbf[...] = w_ref[...].astype(BF16)

    def run(a_ref, s_ref, x_ref, g_ref, o_ref):
        ka = a_ref.shape[-1]
        acc = _dot(a_ref[...], wbf[:ka, :]) + _dot(s_ref[...], wbf[ka:, :])
        o_ref[...] = x_ref[...] + g_ref[...] * acc

    _on_streams(functools.partial(run, ap_ref, sp_ref, xp_ref, gp_ref, op_ref),
                functools.partial(run, as_ref, ss_ref, xs_ref, gs_ref, os_ref))


def mm_outproj(attn, ssm, w, layer, x, g, tm=1024, tn=512):
    b, t, ka = attn[0].shape
    ms = attn[1].shape[0]
    _, k, n = w.shape
    tm = min(tm, t)
    steps, pidx = _streams(b, t, tm)
    ap, as_ = _stream_specs(pidx, tm, ms, ka)
    sp, ss = _stream_specs(pidx, tm, ms, k - ka)
    xp, xs = _stream_specs(pidx, tm, ms, n, tn)
    gp = pl.BlockSpec((None, 1, tn), lambda j, s: (pidx(s)[0], 0, j))
    return pl.pallas_call(
        _outproj_kernel,
        grid=(n // tn, steps + 1),
        in_specs=[ap, sp, as_, ss, pl.BlockSpec((None, k, tn), lambda j, s: (layer, 0, j)), xp, xs, gp, xs],
        out_specs=[xp, xs],
        out_shape=[jax.ShapeDtypeStruct((b, t, n), F32), jax.ShapeDtypeStruct((ms, n), F32)],
        scratch_shapes=[pltpu.VMEM((k, tn), BF16)],
        compiler_params=_cparams(2),
        name="mm_outproj",
    )(attn[0], ssm[0], attn[1], ssm[1], w, x[0], x[1], g[0], g[1])


def _gateup_kernel(hp_ref, hs_ref, wg_ref, wu_ref, wd_ref, op_ref, os_ref, wdbf_ref, wgbf, wubf):
    @pl.when(pl.program_id(1) == 0)
    def _():
        wgbf[...] = wg_ref[...].astype(BF16)
        wubf[...] = wu_ref[...].astype(BF16)

    wdbf_ref[...] = wd_ref[...].astype(BF16)

    def run(h_ref, o_ref):
        h = h_ref[...]
        o_ref[...] = (_silu(_dot(h, wgbf[...])) * _dot(h, wubf[...])).astype(o_ref.dtype)

    _on_streams(functools.partial(run, hp_ref, op_ref), functools.partial(run, hs_ref, os_ref))


def mm_gateup(hp, hs, wg, wu, wd, layer, tm=1024, tn=256):
    b, t, k = hp.shape
    ms = hs.shape[0]
    n = wg.shape[2]
    _, kd, nd = wd.shape
    tm = min(tm, t)
    steps, pidx = _streams(b, t, tm)
    n_tiles = n // tn
    rows = kd // (n_tiles * steps)
    assert n % tn == 0 and rows * n_tiles * steps == kd and rows % (2 * SUBLANES) == 0
    wd_row = lambda j, s: j * steps + jnp.minimum(s, steps - 1)
    wspec = pl.BlockSpec((None, k, tn), lambda j, s: (layer, 0, j))
    return pl.pallas_call(
        _gateup_kernel,
        grid=(n_tiles, steps + 1),
        in_specs=[*_stream_specs(pidx, tm, ms, k), wspec, wspec,
                  pl.BlockSpec((None, rows, nd), lambda j, s: (layer, wd_row(j, s), 0))],
        out_specs=[*_stream_specs(pidx, tm, ms, n, tn),
                   pl.BlockSpec((rows, nd), lambda j, s: (wd_row(j, s), 0))],
        out_shape=[jax.ShapeDtypeStruct((b, t, n), BF16), jax.ShapeDtypeStruct((ms, n), BF16),
                   jax.ShapeDtypeStruct((kd, nd), BF16)],
        scratch_shapes=[pltpu.VMEM((k, tn), BF16), pltpu.VMEM((k, tn), BF16)],
        compiler_params=_cparams(2),
        name="mm_gateup",
    )(hp, hs, wg, wu, wd)


def _down_kernel(h_ref, w_ref, x_ref, g_ref, o_ref):
    o_ref[...] = x_ref[...] + g_ref[...] * _dot(h_ref[...], w_ref[...])


def mm_down(hid, wbf, x, g, tm_cap=512, tn=512):
    b, t, k = hid.shape
    n = wbf.shape[1]
    tm = _row_tile(t, tm_cap)
    gspec = (pl.BlockSpec((None, 1, tn), lambda i, m, j: (i, 0, j)) if g.shape[1] == 1
             else pl.BlockSpec((None, tm, tn), lambda i, m, j: (i, m, j)))
    return pl.pallas_call(
        _down_kernel,
        grid=(b, t // tm, n // tn),
        in_specs=[pl.BlockSpec((None, tm, k), lambda i, m, j: (i, m, 0)),
                  pl.BlockSpec((k, tn), lambda i, m, j: (0, j)),
                  pl.BlockSpec((None, tm, tn), lambda i, m, j: (i, m, j)),
                  gspec],
        out_specs=pl.BlockSpec((None, tm, tn), lambda i, m, j: (i, m, j)),
        out_shape=jax.ShapeDtypeStruct((b, t, n), F32),
        compiler_params=_cparams(3),
        name="mm_down",
    )(hid, wbf, x, g)


def rope_tables(pos):
    half = ROT_DIM // 2
    inv = ROPE_THETA ** (-jnp.arange(0, ROT_DIM, 2, dtype=F32) / ROT_DIM)
    ang = pos.astype(F32)[:, None] * inv[None, :]
    cos, sin = jnp.cos(ang), jnp.sin(ang)
    t = pos.shape[0]
    rest = HEAD_DIM - ROT_DIM
    c = jnp.concatenate([cos, cos, jnp.ones((t, rest), F32)], axis=1)
    s_up = jnp.concatenate([-sin, jnp.zeros((t, half + rest), F32)], axis=1)
    s_dn = jnp.concatenate([jnp.zeros((t, half), F32), sin, jnp.zeros((t, rest), F32)], axis=1)
    reps = LANES // HEAD_DIM
    return tuple(jnp.tile(a, (1, reps)) for a in (c, s_up, s_dn))


def _rope(x, c, s_up, s_dn):
    w = x.shape[-1]
    reps = w // LANES
    half = ROT_DIM // 2
    tile = lambda a: jnp.concatenate([a] * reps, axis=1) if reps > 1 else a
    return (x * tile(c) + pltpu.roll(x, w - half, 1) * tile(s_up) + pltpu.roll(x, half, 1) * tile(s_dn))


def _low_head(shape):
    return lax.broadcasted_iota(jnp.int32, shape, len(shape) - 1) % LANES < HEAD_DIM


def _swa_prompt_kernel(sinks_ref, q_ref, kv_ref, c_ref, su_ref, sd_ref, o_ref, nk_ref, nv_ref, kprev, vprev):
    n = pl.program_id(1)
    last = pl.num_programs(1) - 1
    blk = q_ref.shape[0]

    @pl.when(n == 0)
    def _():
        kprev[...] = jnp.zeros_like(kprev)
        vprev[...] = jnp.zeros_like(vprev)

    c, su, sd = c_ref[...], su_ref[...], sd_ref[...]
    q = _rope(q_ref[...], c, su, sd) * (HEAD_DIM ** -0.5)
    low = _low_head(q.shape)
    q_heads = (jnp.where(low, q, 0.0).astype(BF16), jnp.where(low, 0.0, q).astype(BF16))
    kv = kv_ref[...]
    k = _rope(kv[:, :KV_W], c, su, sd)
    v = kv[:, KV_W:]

    @pl.when(n == last)
    def _():
        nk_ref[...] = k
        nv_ref[...] = v

    kk = jnp.concatenate([kprev[(n + 1) % 2], k], axis=0)
    vv = jnp.concatenate([vprev[(n + 1) % 2], v], axis=0)
    low2 = _low_head((2 * blk, LANES))
    low1 = _low_head((blk, LANES))
    row = lax.broadcasted_iota(jnp.int32, (blk, blk), 0)
    col = lax.broadcasted_iota(jnp.int32, (blk, blk), 1)
    from_prev = col > row
    no_prev = jnp.where(n > 0, 0.0, -jnp.inf)

    def both_halves(x, g):
        tile = x[:, (g // 2) * LANES:(g // 2 + 1) * LANES]
        other = pltpu.roll(tile, HEAD_DIM, 1)
        return (jnp.where(low2, other, tile) if g % 2 else jnp.where(low2, tile, other)).astype(BF16)

    for g in range(N_KV):
        kd, vd = both_halves(kk, g), both_halves(vv, g)
        for jp in range(Q_PER_KV // 2):
            lanes = slice((g * Q_PER_KV // 2 + jp) * LANES, (g * Q_PER_KV // 2 + jp + 1) * LANES)
            outs = []
            for half in range(2):
                h = g * Q_PER_KV + 2 * jp + half
                s = _dot_nt(q_heads[half][:, lanes], kd)
                s = jnp.where(from_prev, s[:, :blk] + no_prev, s[:, blk:])
                sink = sinks_ref[h]
                m = jnp.maximum(jnp.max(s, axis=-1, keepdims=True), sink)
                e = jnp.exp(s - m)
                den = jnp.sum(e, axis=-1, keepdims=True) + jnp.exp(sink - m)
                pe = jnp.concatenate([jnp.where(from_prev, e, 0.0), jnp.where(from_prev, 0.0, e)], axis=1)
                outs.append(_dot(pe.astype(BF16), vd) * (1.0 / den))
            o_ref[:, lanes] = jnp.where(low1, outs[0], outs[1]).astype(o_ref.dtype)
    kprev[n % 2] = k
    vprev[n % 2] = v


def swa_prompt(proj, sinks, tables):
    b, t, _ = proj.shape
    blk = WINDOW
    nb = t // blk
    kv_col = OFF_K // (2 * KV_W)
    tab_spec = pl.BlockSpec((blk, LANES), lambda i, n: (n, 0))
    keep_spec = pl.BlockSpec((None, blk, KV_W), lambda i, n: (i, 0, 0))
    return pl.pallas_call(
        _swa_prompt_kernel,
        grid=(b, nb),
        in_specs=[pl.BlockSpec(memory_space=pltpu.SMEM),
                  pl.BlockSpec((None, blk, Q_W), lambda i, n: (i, n, 0)),
                  pl.BlockSpec((None, blk, 2 * KV_W), lambda i, n: (i, n, kv_col)),
                  tab_spec, tab_spec, tab_spec],
        out_specs=[pl.BlockSpec((None, blk, Q_W), lambda i, n: (i, n, 0)), keep_spec, keep_spec],
        out_shape=[jax.ShapeDtypeStruct((b, t, Q_W), BF16),
                   jax.ShapeDtypeStruct((b, blk, KV_W), F32),
                   jax.ShapeDtypeStruct((b, blk, KV_W), F32)],
        scratch_shapes=[pltpu.VMEM((2, blk, KV_W), F32), pltpu.VMEM((2, blk, KV_W), F32)],
        compiler_params=_cparams(2),
        name="swa_prompt",
    )(sinks, proj, proj, *tables)


def _swa_decode_kernel(t_new, sinks_ref, q_ref, kv_ref, kc_ref, vc_ref, c_ref, su_ref, sd_ref,
                       o_ref, nk_ref, nv_ref, kk, vv):
    rows = q_ref.shape[0]
    wc = kc_ref.shape[0]
    c, su, sd = c_ref[...], su_ref[...], sd_ref[...]
    q = (_rope(q_ref[...], c, su, sd) * (HEAD_DIM ** -0.5)).astype(BF16)
    kv = kv_ref[...]
    k = _rope(kv[:, :KV_W], c, su, sd)
    v = kv[:, KV_W:]
    kc, vc = kc_ref[...], vc_ref[...]

    nk_ref[0:wc - t_new, :] = kc[t_new:, :]
    nv_ref[0:wc - t_new, :] = vc[t_new:, :]
    nk_ref[wc - t_new:wc, :] = k[0:t_new, :]
    nv_ref[wc - t_new:wc, :] = v[0:t_new, :]

    kk[...] = jnp.zeros_like(kk)
    vv[...] = jnp.zeros_like(vv)
    kk[0:wc, :] = kc.astype(BF16)
    vv[0:wc, :] = vc.astype(BF16)
    kk[wc:wc + rows, :] = k.astype(BF16)
    vv[wc:wc + rows, :] = v.astype(BF16)

    nkeys = kk.shape[0]
    nrow = Q_PER_KV * rows
    r = lax.broadcasted_iota(jnp.int32, (nrow, nkeys), 0)
    sj = lax.broadcasted_iota(jnp.int32, (nrow, nkeys), 1)
    tq = r % rows
    mask = jnp.logical_and(jnp.logical_and(sj <= tq + wc, sj > tq + wc - WINDOW), sj < wc + t_new)
    rcol = lax.broadcasted_iota(jnp.int32, (nrow, 1), 0) // rows
    for g in range(N_KV):
        kg = kk[:, g * HEAD_DIM:(g + 1) * HEAD_DIM]
        vg = vv[:, g * HEAD_DIM:(g + 1) * HEAD_DIM]
        qs = jnp.concatenate([q[:, (g * Q_PER_KV + hh) * HEAD_DIM:(g * Q_PER_KV + hh + 1) * HEAD_DIM]
                              for hh in range(Q_PER_KV)], axis=0)
        sink = jnp.zeros((nrow, 1), F32)
        for hh in range(Q_PER_KV):
            sink = jnp.where(rcol == hh, sinks_ref[g * Q_PER_KV + hh], sink)
        s = jnp.where(mask, _dot_nt(qs, kg), -jnp.inf)
        m = jnp.maximum(jnp.max(s, axis=-1, keepdims=True), sink)
        e = jnp.exp(s - m)
        p = e / (jnp.sum(e, axis=-1, keepdims=True) + jnp.exp(sink - m))
        og = _dot(p.astype(BF16), vg)
        for hh in range(Q_PER_KV):
            h = g * Q_PER_KV + hh
            o_ref[:, h * HEAD_DIM:(h + 1) * HEAD_DIM] = og[hh * rows:(hh + 1) * rows, :].astype(o_ref.dtype)


def swa_decode(proj, sinks, cache_k, cache_v, layer, tables, t_new):
    b, rows, _ = proj.shape
    wc = cache_k.shape[2]
    kv_col = OFF_K // (2 * KV_W)
    tab_spec = pl.BlockSpec((rows, LANES), lambda i: (0, 0))
    cache_in = pl.BlockSpec((None, None, wc, KV_W), lambda i: (layer, i, 0, 0))
    cache_out = pl.BlockSpec((None, wc, KV_W), lambda i: (i, 0, 0))
    nkeys = 2 * WINDOW
    return pl.pallas_call(
        functools.partial(_swa_decode_kernel, t_new),
        grid=(b,),
        in_specs=[pl.BlockSpec(memory_space=pltpu.SMEM),
                  pl.BlockSpec((None, rows, Q_W), lambda i: (i, 0, 0)),
                  pl.BlockSpec((None, rows, 2 * KV_W), lambda i: (i, 0, kv_col)),
                  cache_in, cache_in, tab_spec, tab_spec, tab_spec],
        out_specs=[pl.BlockSpec((None, rows, Q_W), lambda i: (i, 0, 0)), cache_out, cache_out],
        out_shape=[jax.ShapeDtypeStruct((b, rows, Q_W), BF16),
                   jax.ShapeDtypeStruct((b, wc, KV_W), F32),
                   jax.ShapeDtypeStruct((b, wc, KV_W), F32)],
        scratch_shapes=[pltpu.VMEM((nkeys, KV_W), BF16), pltpu.VMEM((nkeys, KV_W), BF16)],
        compiler_params=_cparams(1),
        name="swa_decode",
    )(sinks, proj, proj, cache_k, cache_v, *tables)


def spread_matrices():
    src = jnp.arange(LANES)
    head = src % N_SSM_HEADS
    live = src < SPLIT_PARTS * N_SSM_HEADS
    to_x = (head[:, None] == (jnp.arange(D_SSM) // SSM_HEAD_DIM)[None, :]) & live[:, None]
    to_seg = (head[:, None] == (jnp.arange(N_SSM_HEADS * CHUNK) // CHUNK)[None, :]) & live[:, None]
    return to_x.astype(BF16), to_seg.astype(BF16)


def _three_copies(x):
    return x + pltpu.roll(x, N_SSM_HEADS, 1) + pltpu.roll(x, 2 * N_SSM_HEADS, 1)


def _split_parts(x3, part):
    hi = x3.astype(BF16).astype(F32)
    r1 = x3 - hi
    mid = r1.astype(BF16).astype(F32)
    lo = r1 - mid
    return jnp.where(part == 0, hi, jnp.where(part == 1, mid, jnp.where(part == 2, lo, 0.0))).astype(BF16)


def _ssd_kernel(valid, proj_ref, cinit_ref, h0_ref, cw_ref, cb_ref, dtb_ref, alog_ref, dsk_ref, nw_ref,
                tox_ref, toseg_ref, o_ref, tail_ref, hn_ref, xp, ybuf):
    cidx = pl.program_id(1)
    qin = proj_ref.shape[0]
    q = ybuf.shape[0]
    pad = q - qin
    hist = SUBLANES

    @pl.when(cidx == 0)
    def _():
        xp[0:hist, :] = cinit_ref[...]
        hn_ref[...] = h0_ref[...]

    blk = proj_ref[...]
    xp[hist:hist + qin, :] = blk[:, OFF_XBC:OFF_DT]
    if pad:
        xp[hist + qin:, :] = jnp.zeros((pad, CONV_DIM), F32)
    tail_ref[...] = xp[valid:valid + hist, :]

    cw = cw_ref[...]
    conv = cb_ref[...] + cw[CONV_W - 1:CONV_W, :] * xp[hist:hist + q, :]
    for j in range(CONV_W - 1):
        shift = CONV_W - 1 - j
        conv = conv + cw[j:j + 1, :] * xp[hist - shift:hist - shift + q, :]
    xc = _silu(conv)
    xp[0:hist, :] = xp[q:q + hist, :]

    xs = xc[:, :D_SSM]
    gw = D_STATE
    bmat = xc[:, D_SSM:D_SSM + N_SSM_GROUPS * gw].astype(BF16)
    cmat = xc[:, D_SSM + N_SSM_GROUPS * gw:].astype(BF16)

    lane = lax.broadcasted_iota(jnp.int32, (q, LANES), 1)
    rowi = lax.broadcasted_iota(jnp.int32, (q, LANES), 0)
    part = lane // N_SSM_HEADS
    dt_raw = blk[:, OFF_DT:OFF_DT + LANES]
    if pad:
        dt_raw = jnp.concatenate([dt_raw, jnp.zeros((pad, LANES), F32)], axis=0)
    dt_raw = _three_copies(jnp.where(lane < N_SSM_HEADS, dt_raw, 0.0))
    xdtb = dt_raw + dtb_ref[...]
    dt = jnp.maximum(xdtb, 0.0) + jnp.log1p(jnp.exp(-jnp.abs(xdtb)))
    dt = jnp.where(jnp.logical_and(rowi < valid, part < SPLIT_PARTS), dt, 0.0)
    a = dt * (-jnp.exp(alog_ref[...]))

    li = lax.broadcasted_iota(jnp.int32, (q, q), 0)
    si = lax.broadcasted_iota(jnp.int32, (q, q), 1)
    causal = li >= si
    tri = causal.astype(BF16)
    triu = (li <= si).astype(BF16)
    pa = _split_parts(a, part)
    c3 = _dot(tri, pa)
    acum = c3
    for k in range(1, LANES // N_SSM_HEADS):
        acum = acum + pltpu.roll(c3, k * N_SSM_HEADS, 1)
    ct = _dot_tn(pa, triu)
    acum_t = ct[0:N_SSM_HEADS] + ct[N_SSM_HEADS:2 * N_SSM_HEADS] + ct[2 * N_SSM_HEADS:3 * N_SSM_HEADS]
    alast = acum[q - 1:q, :]
    e_acum = jnp.exp(acum)
    e_chunk = e_acum[q - 1:q, :]

    tox = tox_ref[...]
    dtx = _dot(_split_parts(dt, part), tox)
    e_end_x = _dot(_split_parts(jnp.exp(alast - acum), part), tox)
    e_acum_x = _dot(_split_parts(e_acum, part), tox)
    pacum = _split_parts(acum, part)

    xdt = xs * dtx
    low = _low_head(xdt.shape)
    xdt_heads = (jnp.where(low, xdt, 0.0).astype(BF16), jnp.where(low, 0.0, xdt).astype(BF16))
    xe = (xdt * e_end_x).astype(BF16)
    ybuf[...] = dsk_ref[...] * xs

    gsz = SSM_R * SSM_HEAD_DIM
    for g in range(N_SSM_GROUPS):
        bg = bmat[:, g * gw:(g + 1) * gw]
        cg = cmat[:, g * gw:(g + 1) * gw]
        glanes = slice(g * gsz, (g + 1) * gsz)
        grows = slice(g * gsz, (g + 1) * gsz)
        cb = _dot_nt(cg, bg)
        hold = hn_ref[grows, :]
        ybuf[:, glanes] += _dot_nt(cg, hold.astype(BF16)) * e_acum_x[:, glanes]
        seg = _dot(pacum, toseg_ref[:, g * SSM_R * q:(g + 1) * SSM_R * q])
        for jp in range(SSM_R // 2):
            ms = []
            for half in range(2):
                r = 2 * jp + half
                h = g * SSM_R + r
                d = seg[:, r * q:(r + 1) * q] - acum_t[h:h + 1, :]
                ms.append((cb * jnp.exp(jnp.where(causal, d, -jnp.inf))).astype(BF16))
            plane = slice(g * gsz + jp * LANES, g * gsz + (jp + 1) * LANES)
            rhs = jnp.concatenate([xdt_heads[0][:, plane], xdt_heads[1][:, plane]], axis=0)
            ybuf[:, plane] += _dot(jnp.concatenate(ms, axis=1), rhs)
        st = _dot_tn(xe[:, glanes], bg)
        for r in range(SSM_R):
            h = g * SSM_R + r
            rows = slice(h * SSM_HEAD_DIM, (h + 1) * SSM_HEAD_DIM)
            hn_ref[rows, :] = (hold[r * SSM_HEAD_DIM:(r + 1) * SSM_HEAD_DIM, :] * e_chunk[:, h:h + 1]
                               + st[r * SSM_HEAD_DIM:(r + 1) * SSM_HEAD_DIM, :])

    z = blk[:, OFF_Z:OFF_XBC]
    gated = ybuf[0:qin, :] * _silu(z)
    nsz = D_SSM // N_SSM_GROUPS
    nw = nw_ref[...]
    for g in range(N_SSM_GROUPS):
        gg = gated[:, g * nsz:(g + 1) * nsz]
        gg = gg * lax.rsqrt(jnp.mean(gg * gg, axis=-1, keepdims=True) + EPS)
        o_ref[:, g * nsz:(g + 1) * nsz] = (gg * nw[:, g * nsz:(g + 1) * nsz]).astype(o_ref.dtype)


def ssd_mixer(proj, conv_init, h0, layer, conv_w, conv_b, dt_bias, a_log, d_skip, norm_w, spread, qin, valid):
    b, t, _ = proj.shape
    nc = t // qin
    copies = lambda a: jnp.concatenate([a] * SPLIT_PARTS + [jnp.zeros_like(a)]).reshape(1, LANES)
    const2 = lambda shape: pl.BlockSpec(shape, lambda i, c: (0, 0))
    state_in = pl.BlockSpec((None, None, STATE_ROWS, D_STATE), lambda i, c: (layer, i, 0, 0))
    state_out = pl.BlockSpec((None, STATE_ROWS, D_STATE), lambda i, c: (i, 0, 0))
    hist_spec = pl.BlockSpec((None, SUBLANES, CONV_DIM), lambda i, c: (i, 0, 0))
    to_x, to_seg = spread
    return pl.pallas_call(
        functools.partial(_ssd_kernel, valid),
        grid=(b, nc),
        in_specs=[pl.BlockSpec((None, qin, P_PAD), lambda i, c: (i, c, 0)),
                  hist_spec, state_in,
                  const2((CONV_W, CONV_DIM)), const2((1, CONV_DIM)), const2((1, LANES)),
                  const2((1, LANES)), const2((1, D_SSM)), const2((1, D_SSM)),
                  const2(to_x.shape), const2(to_seg.shape)],
        out_specs=[pl.BlockSpec((None, qin, D_SSM), lambda i, c: (i, c, 0)), hist_spec, state_out],
        out_shape=[jax.ShapeDtypeStruct((b, t, D_SSM), BF16),
                   jax.ShapeDtypeStruct((b, SUBLANES, CONV_DIM), F32),
                   jax.ShapeDtypeStruct((b, STATE_ROWS, D_STATE), F32)],
        scratch_shapes=[pltpu.VMEM((SUBLANES + CHUNK, CONV_DIM), F32), pltpu.VMEM((CHUNK, D_SSM), F32)],
        compiler_params=_cparams(2),
        name="ssd_mixer",
    )(proj, conv_init, h0, conv_w, conv_b.reshape(1, -1), copies(dt_bias), copies(a_log),
      jnp.repeat(d_skip, SSM_HEAD_DIM).reshape(1, -1), norm_w.reshape(1, -1), to_x, to_seg)


def _layer(xp, xs, mod_p, mod_s, weights, layer, attn_p, attn_s, conv_init_s, h0_s, spread, zeros_p, t_new):
    (norm1_w, w_in_t, conv_w, conv_b, dt_bias, a_log, d_skip, ssm_norm_w, w_out, norm2_w, w_gate, w_up,
     w_down) = weights
    ms = xs.shape[0]
    bs = ms // SAMPLE_ROWS
    as3 = lambda a: a[None]
    hp = norm_mod(xp, norm1_w[layer], mod_p[1], mod_p[0])
    hs = norm_mod(as3(xs), norm1_w[layer], as3(mod_s[1]), as3(mod_s[0]))[0]
    proj_p, proj_s = mm_inproj(hp, hs, w_in_t, layer)
    proj_s = proj_s.reshape(bs, SAMPLE_ROWS, P_PAD)

    ssd = functools.partial(ssd_mixer, conv_w=conv_w[layer], conv_b=conv_b[layer], dt_bias=dt_bias[layer],
                            a_log=a_log[layer], d_skip=d_skip[layer], norm_w=ssm_norm_w[layer], spread=spread)
    oa_p, k_p, v_p = attn_p(proj_p)
    os_p, tail_p, hn_p = ssd(proj_p, zeros_p[0], zeros_p[1], 0, qin=CHUNK, valid=CHUNK)
    oa_s, k_s, v_s = attn_s(proj_s)
    os_s, tail_s, hn_s = ssd(proj_s, conv_init_s, h0_s, layer, qin=SAMPLE_ROWS, valid=t_new)

    xp, xs = mm_outproj((oa_p, oa_s.reshape(ms, Q_W)), (os_p, os_s.reshape(ms, D_SSM)), w_out, layer,
                        (xp, xs), (mod_p[2], mod_s[2]))
    h2p = norm_mod(xp, norm2_w[layer], mod_p[4], mod_p[3])
    h2s = norm_mod(as3(xs), norm2_w[layer], as3(mod_s[4]), as3(mod_s[3]))[0]
    hid_p, hid_s, wdown_bf = mm_gateup(h2p, h2s, w_gate, w_up, w_down, layer)
    xp = mm_down(hid_p, wdown_bf, xp, mod_p[5])
    xs = mm_down(as3(hid_s), wdown_bf, as3(xs), as3(mod_s[5]))[0]

    keep = SUBLANES - (CONV_W - 1)
    state = lambda h: h.reshape(-1, N_SSM_HEADS, SSM_HEAD_DIM, D_STATE)
    heads = lambda a: a.reshape(a.shape[0], a.shape[1], N_KV, HEAD_DIM)
    return xp, xs, (heads(k_p), heads(v_p), tail_p[:, keep:], state(hn_p),
                    heads(k_s), heads(v_s), tail_s[:, keep:], state(hn_s))


def kernel(x_prompt, x_sample, cache_k, cache_v, state_conv, state_ssm, c_prompt, c_sample, w_ada, b_ada,
           norm1_w, w_in, conv_w, conv_b, dt_bias, a_log, d_skip, ssm_norm_w, sinks, w_out, norm2_w,
           w_gate, w_up, w_down, final_norm_w):
    depth = w_ada.shape[0]
    bp, tp, d = x_prompt.shape
    bs, ts, _ = x_sample.shape
    wc = cache_k.shape[2]
    assert ts <= SAMPLE_ROWS and tp % CHUNK == 0 and ts % CHUNK != 0

    c_all = jnp.concatenate([c_prompt, c_sample, jnp.zeros((ADA_ROWS - bp - bs, d), F32)], axis=0)
    mod_all = ada_mod(c_all, w_ada, b_ada)

    tab_p = rope_tables(jnp.arange(tp, dtype=jnp.int32))
    tab_s = rope_tables(PAST_LEN + jnp.arange(SAMPLE_ROWS, dtype=jnp.int32))
    spread = spread_matrices()
    weights = (norm1_w, jnp.swapaxes(w_in, 1, 2), conv_w, conv_b, dt_bias, a_log, d_skip, ssm_norm_w, w_out,
               norm2_w, w_gate, w_up, w_down)

    xp = x_prompt
    xs = jnp.pad(x_sample, ((0, 0), (0, SAMPLE_ROWS - ts), (0, 0))).reshape(bs * SAMPLE_ROWS, d)
    zeros_p = (jnp.zeros((bp, SUBLANES, CONV_DIM), F32), jnp.zeros((1, bp, STATE_ROWS, D_STATE), F32))
    h0_s = state_ssm.reshape(depth, bs, STATE_ROWS, D_STATE)
    ck = cache_k.reshape(depth, bs, wc, KV_W)
    cv = cache_v.reshape(depth, bs, wc, KV_W)

    outs = []
    for l in range(depth):
        mod_p = [mod_all[l, :bp, i * d:(i + 1) * d][:, None, :] for i in range(6)]
        mod_s = [jnp.repeat(mod_all[l, bp:bp + bs, i * d:(i + 1) * d], SAMPLE_ROWS, axis=0) for i in range(6)]
        attn_p = functools.partial(swa_prompt, sinks=sinks[l], tables=tab_p)
        attn_s = functools.partial(swa_decode, sinks=sinks[l], cache_k=ck, cache_v=cv, layer=l, tables=tab_s,
                                   t_new=ts)
        conv_init_s = jnp.pad(state_conv[l], ((0, 0), (SUBLANES - (CONV_W - 1), 0), (0, 0)))
        xp, xs, layer_outs = _layer(xp, xs, mod_p, mod_s, weights, l, attn_p, attn_s, conv_init_s, h0_s, spread,
                                    zeros_p, ts)
        outs.append(layer_outs)

    y_prompt = final_norm(xp, final_norm_w)
    y_sample = final_norm(xs[None], final_norm_w).reshape(bs, SAMPLE_ROWS, d)[:, :ts]
    return (y_prompt, y_sample) + tuple(jnp.stack(o) for o in zip(*outs))
```

```python
import functools

import jax
import jax.numpy as jnp
from jax import lax
from jax.experimental import pallas as pl
from jax.experimental.pallas import tpu as pltpu

F32 = jnp.float32
BF16 = jnp.bfloat16

D_MODEL = 4096
HEAD_DIM = 64
N_HEADS = 32
N_KV = 4
Q_PER_KV = N_HEADS // N_KV
WINDOW = 128
ROT_DIM = 16
ROPE_THETA = 500000.0
PAST_LEN = 16384
D_SSM = 2048
N_SSM_HEADS = 32
SSM_HEAD_DIM = 64
N_SSM_GROUPS = 4
SSM_R = N_SSM_HEADS // N_SSM_GROUPS
D_STATE = 128
CONV_W = 4
CONV_DIM = D_SSM + 2 * N_SSM_GROUPS * D_STATE
CHUNK = 128
D_FF = 11008
EPS = 1e-6
Q_W = N_HEADS * HEAD_DIM
KV_W = N_KV * HEAD_DIM
OFF_K = Q_W
OFF_V = Q_W + KV_W
OFF_Z = Q_W + 2 * KV_W
OFF_XBC = OFF_Z + D_SSM
OFF_DT = OFF_XBC + CONV_DIM
P_IN = OFF_DT + N_SSM_HEADS

LANES = 128
SUBLANES = 8
VMEM_LIMIT_BYTES = 56 * 1024 * 1024

P_PAD = -(-P_IN // LANES) * LANES
SAMPLE_ROWS = 8
ADA_ROWS = 40
SPLIT_PARTS = 3
STATE_ROWS = N_SSM_HEADS * SSM_HEAD_DIM


def _cparams(n_axes):
    return pltpu.CompilerParams(dimension_semantics=("arbitrary",) * n_axes, vmem_limit_bytes=VMEM_LIMIT_BYTES)


def _silu(x):
    return x * jax.nn.sigmoid(x)


def _dot(a, b):
    return jnp.dot(a, b, preferred_element_type=F32)


def _dot_nt(a, b):
    return lax.dot_general(a, b, (((1,), (1,)), ((), ())), preferred_element_type=F32)


def _dot_tn(a, b):
    return lax.dot_general(a, b, (((0,), (0,)), ((), ())), preferred_element_type=F32)


def _ada_kernel(c_ref, w_ref, b_ref, o_ref):
    x = _silu(c_ref[...]).astype(BF16)
    w = w_ref[...].astype(BF16)
    o_ref[...] = _dot(x, w) + b_ref[...]


def ada_mod(c_all, w_ada, b_ada, tn=1024):
    depth, d, n = w_ada.shape
    rows = c_all.shape[0]
    return pl.pallas_call(
        _ada_kernel,
        grid=(depth, n // tn),
        in_specs=[
            pl.BlockSpec((rows, d), lambda l, j: (0, 0)),
            pl.BlockSpec((None, d, tn), lambda l, j: (l, 0, j)),
            pl.BlockSpec((None, 1, tn), lambda l, j: (l, 0, j)),
        ],
        out_specs=pl.BlockSpec((None, rows, tn), lambda l, j: (l, 0, j)),
        out_shape=jax.ShapeDtypeStruct((depth, rows, n), F32),
        compiler_params=_cparams(2),
        name="ada_mod",
    )(c_all, w_ada, b_ada.reshape(depth, 1, n))


def _norm_mod_kernel(x_ref, nw_ref, sc_ref, sh_ref, o_ref):
    x = x_ref[...]
    y = x * lax.rsqrt(jnp.mean(x * x, axis=-1, keepdims=True) + EPS) * nw_ref[...]
    o_ref[...] = (y * (1.0 + sc_ref[...]) + sh_ref[...]).astype(o_ref.dtype)


def _norm_kernel(x_ref, nw_ref, o_ref):
    x = x_ref[...]
    o_ref[...] = x * lax.rsqrt(jnp.mean(x * x, axis=-1, keepdims=True) + EPS) * nw_ref[...]


def _row_tile(t, cap):
    tt = min(t, cap)
    assert t % tt == 0
    return tt


def norm_mod(x, nw, sc, sh, tt_cap=512):
    b, t, d = x.shape
    tt = _row_tile(t, tt_cap)
    mt = sc.shape[1]
    mod_spec = (pl.BlockSpec((None, 1, d), lambda i, j: (i, 0, 0)) if mt == 1
                else pl.BlockSpec((None, tt, d), lambda i, j: (i, j, 0)))
    return pl.pallas_call(
        _norm_mod_kernel,
        grid=(b, t // tt),
        in_specs=[pl.BlockSpec((None, tt, d), lambda i, j: (i, j, 0)),
                  pl.BlockSpec((1, d), lambda i, j: (0, 0)), mod_spec, mod_spec],
        out_specs=pl.BlockSpec((None, tt, d), lambda i, j: (i, j, 0)),
        out_shape=jax.ShapeDtypeStruct((b, t, d), BF16),
        compiler_params=_cparams(2),
        name="norm_mod",
    )(x, nw.reshape(1, d), sc, sh)


def final_norm(x, nw, tt_cap=256):
    b, t, d = x.shape
    tt = _row_tile(t, tt_cap)
    return pl.pallas_call(
        _norm_kernel,
        grid=(b, t // tt),
        in_specs=[pl.BlockSpec((None, tt, d), lambda i, j: (i, j, 0)),
                  pl.BlockSpec((1, d), lambda i, j: (0, 0))],
        out_specs=pl.BlockSpec((None, tt, d), lambda i, j: (i, j, 0)),
        out_shape=jax.ShapeDtypeStruct((b, t, d), F32),
        compiler_params=_cparams(2),
        name="final_norm",
    )(x, nw.reshape(1, d))


def _streams(b, t, tm):
    assert t % tm == 0
    per_b = t // tm
    steps = b * per_b

    def prompt_idx(s):
        return s // per_b, s % per_b

    return steps, prompt_idx


def _stream_specs(pidx, tm, ms, width, col=None):
    if col is None:
        return (pl.BlockSpec((None, tm, width), lambda j, s: (*pidx(s), 0)),
                pl.BlockSpec((ms, width), lambda j, s: (0, 0)))
    return (pl.BlockSpec((None, tm, col), lambda j, s: (*pidx(s), j)),
            pl.BlockSpec((ms, col), lambda j, s: (0, j)))


def _on_streams(prompt_fn, sample_fn):
    prompt_fn()
    pl.when(pl.program_id(1) == pl.num_programs(1) - 1)(sample_fn)


def _inproj_kernel(hp_ref, hs_ref, wt_ref, op_ref, os_ref, wbf):
    @pl.when(pl.program_id(1) == 0)
    def _():
        wbf[...] = wt_ref[...].T.astype(BF16)

    def run(h_ref, o_ref):
        o_ref[...] = _dot(h_ref[...], wbf[...])

    _on_streams(functools.partial(run, hp_ref, op_ref), functools.partial(run, hs_ref, os_ref))


def mm_inproj(hp, hs, wt, layer, tm=1024, tn=512):
    b, t, k = hp.shape
    ms = hs.shape[0]
    tm = min(tm, t)
    steps, pidx = _streams(b, t, tm)
    return pl.pallas_call(
        _inproj_kernel,
        grid=(pl.cdiv(P_PAD, tn), steps),
        in_specs=[*_stream_specs(pidx, tm, ms, k),
                  pl.BlockSpec((None, tn, k), lambda j, s: (layer, j, 0))],
        out_specs=list(_stream_specs(pidx, tm, ms, P_PAD, tn)),
        out_shape=[jax.ShapeDtypeStruct((b, t, P_PAD), F32), jax.ShapeDtypeStruct((ms, P_PAD), F32)],
        scratch_shapes=[pltpu.VMEM((k, tn), BF16)],
        compiler_params=_cparams(2),
        name="mm_inproj",
    )(hp, hs, wt)


def _outproj_kernel(ap_ref, sp_ref, as_ref, ss_ref, w_ref, xp_ref, xs_ref, gp_ref, gs_ref, op_ref, os_ref, wbf):
    @pl.when(pl.program_id(1) == 0)
    def _():
        wbf[...] = w_ref[...].astype(BF16)

    def run(a_ref, s_ref, x_ref, g_ref, o_ref):
        ka = a_ref.shape[-1]
        acc = _dot(a_ref[...], wbf[:ka, :]) + _dot(s_ref[...], wbf[ka:, :])
        o_ref[...] = x_ref[...] + g_ref[...] * acc

    _on_streams(functools.partial(run, ap_ref, sp_ref, xp_ref, gp_ref, op_ref),
                functools.partial(run, as_ref, ss_ref, xs_ref, gs_ref, os_ref))


def mm_outproj(attn, ssm, w, layer, x, g, tm=1024, tn=512):
    b, t, ka = attn[0].shape
    ms = attn[1].shape[0]
    _, k, n = w.shape
    tm = min(tm, t)
    steps, pidx = _streams(b, t, tm)
    ap, as_ = _stream_specs(pidx, tm, ms, ka)
    sp, ss = _stream_specs(pidx, tm, ms, k - ka)
    xp, xs = _stream_specs(pidx, tm, ms, n, tn)
    gp = pl.BlockSpec((None, 1, tn), lambda j, s: (pidx(s)[0], 0, j))
    return pl.pallas_call(
        _outproj_kernel,
        grid=(n // tn, steps),
        in_specs=[ap, sp, as_, ss, pl.BlockSpec((None, k, tn), lambda j, s: (layer, 0, j)), xp, xs, gp, xs],
        out_specs=[xp, xs],
        out_shape=[jax.ShapeDtypeStruct((b, t, n), F32), jax.ShapeDtypeStruct((ms, n), F32)],
        scratch_shapes=[pltpu.VMEM((k, tn), BF16)],
        compiler_params=_cparams(2),
        name="mm_outproj",
    )(attn[0], ssm[0], attn[1], ssm[1], w, x[0], x[1], g[0], g[1])


def _gateup_kernel(hp_ref, hs_ref, wg_ref, wu_ref, wd_ref, op_ref, os_ref, wdbf_ref, wgbf, wubf):
    @pl.when(pl.program_id(1) == 0)
    def _():
        wgbf[...] = wg_ref[...].astype(BF16)
        wubf[...] = wu_ref[...].astype(BF16)

    wdbf_ref[...] = wd_ref[...].astype(BF16)

    def run(h_ref, o_ref):
        h = h_ref[...]
        o_ref[...] = (_silu(_dot(h, wgbf[...])) * _dot(h, wubf[...])).astype(o_ref.dtype)

    _on_streams(functools.partial(run, hp_ref, op_ref), functools.partial(run, hs_ref, os_ref))


def mm_gateup(hp, hs, wg, wu, wd, layer, tm=1024, tn=256):
    b, t, k = hp.shape
    ms = hs.shape[0]
    n = wg.shape[2]
    _, kd, nd = wd.shape
    tm = min(tm, t)
    steps, pidx = _streams(b, t, tm)
    n_tiles = n // tn
    rows = kd // (n_tiles * steps)
    assert n % tn == 0 and rows * n_tiles * steps == kd and rows % (2 * SUBLANES) == 0
    wd_row = lambda j, s: j * steps + s
    wspec = pl.BlockSpec((None, k, tn), lambda j, s: (layer, 0, j))
    return pl.pallas_call(
        _gateup_kernel,
        grid=(n_tiles, steps),
        in_specs=[*_stream_specs(pidx, tm, ms, k), wspec, wspec,
                  pl.BlockSpec((None, rows, nd), lambda j, s: (layer, wd_row(j, s), 0))],
        out_specs=[*_stream_specs(pidx, tm, ms, n, tn),
                   pl.BlockSpec((rows, nd), lambda j, s: (wd_row(j, s), 0))],
        out_shape=[jax.ShapeDtypeStruct((b, t, n), BF16), jax.ShapeDtypeStruct((ms, n), BF16),
                   jax.ShapeDtypeStruct((kd, nd), BF16)],
        scratch_shapes=[pltpu.VMEM((k, tn), BF16), pltpu.VMEM((k, tn), BF16)],
        compiler_params=_cparams(2),
        name="mm_gateup",
    )(hp, hs, wg, wu, wd)


def _down_kernel(h_ref, w_ref, x_ref, g_ref, o_ref):
    o_ref[...] = x_ref[...] + g_ref[...] * _dot(h_ref[...], w_ref[...])


def mm_down(hid, wbf, x, g, tm_cap=512, tn=512):
    b, t, k = hid.shape
    n = wbf.shape[1]
    tm = _row_tile(t, tm_cap)
    gspec = (pl.BlockSpec((None, 1, tn), lambda i, m, j: (i, 0, j)) if g.shape[1] == 1
             else pl.BlockSpec((None, tm, tn), lambda i, m, j: (i, m, j)))
    return pl.pallas_call(
        _down_kernel,
        grid=(b, t // tm, n // tn),
        in_specs=[pl.BlockSpec((None, tm, k), lambda i, m, j: (i, m, 0)),
                  pl.BlockSpec((k, tn), lambda i, m, j: (0, j)),
                  pl.BlockSpec((None, tm, tn), lambda i, m, j: (i, m, j)),
                  gspec],
        out_specs=pl.BlockSpec((None, tm, tn), lambda i, m, j: (i, m, j)),
        out_shape=jax.ShapeDtypeStruct((b, t, n), F32),
        compiler_params=_cparams(3),
        name="mm_down",
    )(hid, wbf, x, g)


def rope_tables(pos):
    half = ROT_DIM // 2
    inv = ROPE_THETA ** (-jnp.arange(0, ROT_DIM, 2, dtype=F32) / ROT_DIM)
    ang = pos.astype(F32)[:, None] * inv[None, :]
    cos, sin = jnp.cos(ang), jnp.sin(ang)
    t = pos.shape[0]
    rest = HEAD_DIM - ROT_DIM
    c = jnp.concatenate([cos, cos, jnp.ones((t, rest), F32)], axis=1)
    s_up = jnp.concatenate([-sin, jnp.zeros((t, half + rest), F32)], axis=1)
    s_dn = jnp.concatenate([jnp.zeros((t, half), F32), sin, jnp.zeros((t, rest), F32)], axis=1)
    reps = LANES // HEAD_DIM
    return tuple(jnp.tile(a, (1, reps)) for a in (c, s_up, s_dn))


def _rope(x, c, s_up, s_dn):
    w = x.shape[-1]
    reps = w // LANES
    half = ROT_DIM // 2
    tile = lambda a: jnp.concatenate([a] * reps, axis=1) if reps > 1 else a
    return (x * tile(c) + pltpu.roll(x, w - half, 1) * tile(s_up) + pltpu.roll(x, half, 1) * tile(s_dn))


def _low_head(shape):
    return lax.broadcasted_iota(jnp.int32, shape, len(shape) - 1) % LANES < HEAD_DIM


def _swa_prompt_kernel(sinks_ref, q_ref, kv_ref, c_ref, su_ref, sd_ref, o_ref, nk_ref, nv_ref, kprev, vprev):
    n = pl.program_id(1)
    last = pl.num_programs(1) - 1
    blk = q_ref.shape[0]

    @pl.when(n == 0)
    def _():
        kprev[...] = jnp.zeros_like(kprev)
        vprev[...] = jnp.zeros_like(vprev)

    c, su, sd = c_ref[...], su_ref[...], sd_ref[...]
    q = _rope(q_ref[...], c, su, sd) * (HEAD_DIM ** -0.5)
    low = _low_head(q.shape)
    q_heads = (jnp.where(low, q, 0.0).astype(BF16), jnp.where(low, 0.0, q).astype(BF16))
    kv = kv_ref[...]
    k = _rope(kv[:, :KV_W], c, su, sd)
    v = kv[:, KV_W:]

    @pl.when(n == last)
    def _():
        nk_ref[...] = k
        nv_ref[...] = v

    kk = jnp.concatenate([kprev[(n + 1) % 2], k], axis=0)
    vv = jnp.concatenate([vprev[(n + 1) % 2], v], axis=0)
    low2 = _low_head((2 * blk, LANES))
    low1 = _low_head((blk, LANES))
    row = lax.broadcasted_iota(jnp.int32, (blk, blk), 0)
    col = lax.broadcasted_iota(jnp.int32, (blk, blk), 1)
    from_prev = col > row
    no_prev = jnp.where(n > 0, 0.0, -jnp.inf)

    def both_halves(x, g):
        tile = x[:, (g // 2) * LANES:(g // 2 + 1) * LANES]
        other = pltpu.roll(tile, HEAD_DIM, 1)
        return (jnp.where(low2, other, tile) if g % 2 else jnp.where(low2, tile, other)).astype(BF16)

    for g in range(N_KV):
        kd, vd = both_halves(kk, g), both_halves(vv, g)
        for jp in range(Q_PER_KV // 2):
            lanes = slice((g * Q_PER_KV // 2 + jp) * LANES, (g * Q_PER_KV // 2 + jp + 1) * LANES)
            outs = []
            for half in range(2):
                h = g * Q_PER_KV + 2 * jp + half
                s = _dot_nt(q_heads[half][:, lanes], kd)
                s = jnp.where(from_prev, s[:, :blk] + no_prev, s[:, blk:])
                sink = sinks_ref[h]
                m = jnp.maximum(jnp.max(s, axis=-1, keepdims=True), sink)
                e = jnp.exp(s - m)
                den = jnp.sum(e, axis=-1, keepdims=True) + jnp.exp(sink - m)
                pe = jnp.concatenate([jnp.where(from_prev, e, 0.0), jnp.where(from_prev, 0.0, e)], axis=1)
                outs.append(_dot(pe.astype(BF16), vd) * (1.0 / den))
            o_ref[:, lanes] = jnp.where(low1, outs[0], outs[1]).astype(o_ref.dtype)
    kprev[n % 2] = k
    vprev[n % 2] = v


def swa_prompt(proj, sinks, tables):
    b, t, _ = proj.shape
    blk = WINDOW
    nb = t // blk
    kv_col = OFF_K // (2 * KV_W)
    tab_spec = pl.BlockSpec((blk, LANES), lambda i, n: (n, 0))
    keep_spec = pl.BlockSpec((None, blk, KV_W), lambda i, n: (i, 0, 0))
    return pl.pallas_call(
        _swa_prompt_kernel,
        grid=(b, nb),
        in_specs=[pl.BlockSpec(memory_space=pltpu.SMEM),
                  pl.BlockSpec((None, blk, Q_W), lambda i, n: (i, n, 0)),
                  pl.BlockSpec((None, blk, 2 * KV_W), lambda i, n: (i, n, kv_col)),
                  tab_spec, tab_spec, tab_spec],
        out_specs=[pl.BlockSpec((None, blk, Q_W), lambda i, n: (i, n, 0)), keep_spec, keep_spec],
        out_shape=[jax.ShapeDtypeStruct((b, t, Q_W), BF16),
                   jax.ShapeDtypeStruct((b, blk, KV_W), F32),
                   jax.ShapeDtypeStruct((b, blk, KV_W), F32)],
        scratch_shapes=[pltpu.VMEM((2, blk, KV_W), F32), pltpu.VMEM((2, blk, KV_W), F32)],
        compiler_params=_cparams(2),
        name="swa_prompt",
    )(sinks, proj, proj, *tables)


def _swa_decode_kernel(t_new, sinks_ref, q_ref, kv_ref, kc_ref, vc_ref, c_ref, su_ref, sd_ref,
                       o_ref, nk_ref, nv_ref, kk, vv):
    rows = q_ref.shape[0]
    wc = kc_ref.shape[0]
    c, su, sd = c_ref[...], su_ref[...], sd_ref[...]
    q = (_rope(q_ref[...], c, su, sd) * (HEAD_DIM ** -0.5)).astype(BF16)
    kv = kv_ref[...]
    k = _rope(kv[:, :KV_W], c, su, sd)
    v = kv[:, KV_W:]
    kc, vc = kc_ref[...], vc_ref[...]

    nk_ref[0:wc - t_new, :] = kc[t_new:, :]
    nv_ref[0:wc - t_new, :] = vc[t_new:, :]
    nk_ref[wc - t_new:wc, :] = k[0:t_new, :]
    nv_ref[wc - t_new:wc, :] = v[0:t_new, :]

    kk[...] = jnp.zeros_like(kk)
    vv[...] = jnp.zeros_like(vv)
    kk[0:wc, :] = kc.astype(BF16)
    vv[0:wc, :] = vc.astype(BF16)
    kk[wc:wc + rows, :] = k.astype(BF16)
    vv[wc:wc + rows, :] = v.astype(BF16)

    nkeys = kk.shape[0]
    nrow = Q_PER_KV * rows
    r = lax.broadcasted_iota(jnp.int32, (nrow, nkeys), 0)
    sj = lax.broadcasted_iota(jnp.int32, (nrow, nkeys), 1)
    tq = r % rows
    mask = jnp.logical_and(jnp.logical_and(sj <= tq + wc, sj > tq + wc - WINDOW), sj < wc + t_new)
    rcol = lax.broadcasted_iota(jnp.int32, (nrow, 1), 0) // rows
    for g in range(N_KV):
        kg = kk[:, g * HEAD_DIM:(g + 1) * HEAD_DIM]
        vg = vv[:, g * HEAD_DIM:(g + 1) * HEAD_DIM]
        qs = jnp.concatenate([q[:, (g * Q_PER_KV + hh) * HEAD_DIM:(g * Q_PER_KV + hh + 1) * HEAD_DIM]
                              for hh in range(Q_PER_KV)], axis=0)
        sink = jnp.zeros((nrow, 1), F32)
        for hh in range(Q_PER_KV):
            sink = jnp.where(rcol == hh, sinks_ref[g * Q_PER_KV + hh], sink)
        s = jnp.where(mask, _dot_nt(qs, kg), -jnp.inf)
        m = jnp.maximum(jnp.max(s, axis=-1, keepdims=True), sink)
        e = jnp.exp(s - m)
        p = e / (jnp.sum(e, axis=-1, keepdims=True) + jnp.exp(sink - m))
        og = _dot(p.astype(BF16), vg)
        for hh in range(Q_PER_KV):
            h = g * Q_PER_KV + hh
            o_ref[:, h * HEAD_DIM:(h + 1) * HEAD_DIM] = og[hh * rows:(hh + 1) * rows, :].astype(o_ref.dtype)


def swa_decode(proj, sinks, cache_k, cache_v, layer, tables, t_new):
    b, rows, _ = proj.shape
    wc = cache_k.shape[2]
    kv_col = OFF_K // (2 * KV_W)
    tab_spec = pl.BlockSpec((rows, LANES), lambda i: (0, 0))
    cache_in = pl.BlockSpec((None, None, wc, KV_W), lambda i: (layer, i, 0, 0))
    cache_out = pl.BlockSpec((None, wc, KV_W), lambda i: (i, 0, 0))
    nkeys = 2 * WINDOW
    return pl.pallas_call(
        functools.partial(_swa_decode_kernel, t_new),
        grid=(b,),
        in_specs=[pl.BlockSpec(memory_space=pltpu.SMEM),
                  pl.BlockSpec((None, rows, Q_W), lambda i: (i, 0, 0)),
                  pl.BlockSpec((None, rows, 2 * KV_W), lambda i: (i, 0, kv_col)),
                  cache_in, cache_in, tab_spec, tab_spec, tab_spec],
        out_specs=[pl.BlockSpec((None, rows, Q_W), lambda i: (i, 0, 0)), cache_out, cache_out],
        out_shape=[jax.ShapeDtypeStruct((b, rows, Q_W), BF16),
                   jax.ShapeDtypeStruct((b, wc, KV_W), F32),
                   jax.ShapeDtypeStruct((b, wc, KV_W), F32)],
        scratch_shapes=[pltpu.VMEM((nkeys, KV_W), BF16), pltpu.VMEM((nkeys, KV_W), BF16)],
        compiler_params=_cparams(1),
        name="swa_decode",
    )(sinks, proj, proj, cache_k, cache_v, *tables)


def spread_matrices():
    src = jnp.arange(LANES)
    head = src % N_SSM_HEADS
    live = src < SPLIT_PARTS * N_SSM_HEADS
    to_x = (head[:, None] == (jnp.arange(D_SSM) // SSM_HEAD_DIM)[None, :]) & live[:, None]
    to_seg = (head[:, None] == (jnp.arange(N_SSM_HEADS * CHUNK) // CHUNK)[None, :]) & live[:, None]
    return to_x.astype(BF16), to_seg.astype(BF16)


def _three_copies(x):
    return x + pltpu.roll(x, N_SSM_HEADS, 1) + pltpu.roll(x, 2 * N_SSM_HEADS, 1)


def _split_parts(x3, part):
    hi = x3.astype(BF16).astype(F32)
    r1 = x3 - hi
    mid = r1.astype(BF16).astype(F32)
    lo = r1 - mid
    return jnp.where(part == 0, hi, jnp.where(part == 1, mid, jnp.where(part == 2, lo, 0.0))).astype(BF16)


def _ssd_kernel(valid, proj_ref, cinit_ref, h0_ref, cw_ref, cb_ref, dtb_ref, alog_ref, dsk_ref, nw_ref,
                tox_ref, toseg_ref, o_ref, tail_ref, hn_ref, xp, ybuf):
    cidx = pl.program_id(1)
    qin = proj_ref.shape[0]
    q = ybuf.shape[0]
    pad = q - qin
    hist = SUBLANES

    @pl.when(cidx == 0)
    def _():
        xp[0:hist, :] = cinit_ref[...]
        hn_ref[...] = h0_ref[...]

    blk = proj_ref[...]
    xp[hist:hist + qin, :] = blk[:, OFF_XBC:OFF_DT]
    if pad:
        xp[hist + qin:, :] = jnp.zeros((pad, CONV_DIM), F32)
    tail_ref[...] = xp[valid:valid + hist, :]

    cw = cw_ref[...]
    conv = cb_ref[...] + cw[CONV_W - 1:CONV_W, :] * xp[hist:hist + q, :]
    for j in range(CONV_W - 1):
        shift = CONV_W - 1 - j
        conv = conv + cw[j:j + 1, :] * xp[hist - shift:hist - shift + q, :]
    xc = _silu(conv)
    xp[0:hist, :] = xp[q:q + hist, :]

    xs = xc[:, :D_SSM]
    gw = D_STATE
    bmat = xc[:, D_SSM:D_SSM + N_SSM_GROUPS * gw].astype(BF16)
    cmat = xc[:, D_SSM + N_SSM_GROUPS * gw:].astype(BF16)

    lane = lax.broadcasted_iota(jnp.int32, (q, LANES), 1)
    rowi = lax.broadcasted_iota(jnp.int32, (q, LANES), 0)
    part = lane // N_SSM_HEADS
    dt_raw = blk[:, OFF_DT:OFF_DT + LANES]
    if pad:
        dt_raw = jnp.concatenate([dt_raw, jnp.zeros((pad, LANES), F32)], axis=0)
    dt_raw = _three_copies(jnp.where(lane < N_SSM_HEADS, dt_raw, 0.0))
    xdtb = dt_raw + dtb_ref[...]
    dt = jnp.maximum(xdtb, 0.0) + jnp.log1p(jnp.exp(-jnp.abs(xdtb)))
    dt = jnp.where(jnp.logical_and(rowi < valid, part < SPLIT_PARTS), dt, 0.0)
    a = dt * (-jnp.exp(alog_ref[...]))

    li = lax.broadcasted_iota(jnp.int32, (q, q), 0)
    si = lax.broadcasted_iota(jnp.int32, (q, q), 1)
    causal = li >= si
    tri = causal.astype(BF16)
    triu = (li <= si).astype(BF16)
    pa = _split_parts(a, part)
    c3 = _dot(tri, pa)
    acum = c3
    for k in range(1, LANES // N_SSM_HEADS):
        acum = acum + pltpu.roll(c3, k * N_SSM_HEADS, 1)
    ct = _dot_tn(pa, triu)
    acum_t = ct[0:N_SSM_HEADS] + ct[N_SSM_HEADS:2 * N_SSM_HEADS] + ct[2 * N_SSM_HEADS:3 * N_SSM_HEADS]
    alast = acum[q - 1:q, :]
    e_acum = jnp.exp(acum)
    e_chunk = e_acum[q - 1:q, :]

    tox = tox_ref[...]
    dtx = _dot(_split_parts(dt, part), tox)
    e_end_x = _dot(_split_parts(jnp.exp(alast - acum), part), tox)
    e_acum_x = _dot(_split_parts(e_acum, part), tox)
    pacum = _split_parts(acum, part)

    xdt = xs * dtx
    low = _low_head(xdt.shape)
    xdt_heads = (jnp.where(low, xdt, 0.0).astype(BF16), jnp.where(low, 0.0, xdt).astype(BF16))
    xe = (xdt * e_end_x).astype(BF16)
    ybuf[...] = dsk_ref[...] * xs

    gsz = SSM_R * SSM_HEAD_DIM
    for g in range(N_SSM_GROUPS):
        bg = bmat[:, g * gw:(g + 1) * gw]
        cg = cmat[:, g * gw:(g + 1) * gw]
        glanes = slice(g * gsz, (g + 1) * gsz)
        grows = slice(g * gsz, (g + 1) * gsz)
        cb = _dot_nt(cg, bg)
        hold = hn_ref[grows, :]
        ybuf[:, glanes] += _dot_nt(cg, hold.astype(BF16)) * e_acum_x[:, glanes]
        seg = _dot(pacum, toseg_ref[:, g * SSM_R * q:(g + 1) * SSM_R * q])
        for jp in range(SSM_R // 2):
            ms = []
            for half in range(2):
                r = 2 * jp + half
                h = g * SSM_R + r
                d = seg[:, r * q:(r + 1) * q] - acum_t[h:h + 1, :]
                ms.append((cb * jnp.exp(jnp.where(causal, d, -jnp.inf))).astype(BF16))
            plane = slice(g * gsz + jp * LANES, g * gsz + (jp + 1) * LANES)
            rhs = jnp.concatenate([xdt_heads[0][:, plane], xdt_heads[1][:, plane]], axis=0)
            ybuf[:, plane] += _dot(jnp.concatenate(ms, axis=1), rhs)
        st = _dot_tn(xe[:, glanes], bg)
        for r in range(SSM_R):
            h = g * SSM_R + r
            rows = slice(h * SSM_HEAD_DIM, (h + 1) * SSM_HEAD_DIM)
            hn_ref[rows, :] = (hold[r * SSM_HEAD_DIM:(r + 1) * SSM_HEAD_DIM, :] * e_chunk[:, h:h + 1]
                               + st[r * SSM_HEAD_DIM:(r + 1) * SSM_HEAD_DIM, :])

    z = blk[:, OFF_Z:OFF_XBC]
    gated = ybuf[0:qin, :] * _silu(z)
    nsz = D_SSM // N_SSM_GROUPS
    nw = nw_ref[...]
    for g in range(N_SSM_GROUPS):
        gg = gated[:, g * nsz:(g + 1) * nsz]
        gg = gg * lax.rsqrt(jnp.mean(gg * gg, axis=-1, keepdims=True) + EPS)
        o_ref[:, g * nsz:(g + 1) * nsz] = (gg * nw[:, g * nsz:(g + 1) * nsz]).astype(o_ref.dtype)


def ssd_mixer(proj, conv_init, h0, layer, conv_w, conv_b, dt_bias, a_log, d_skip, norm_w, spread, qin, valid):
    b, t, _ = proj.shape
    nc = t // qin
    copies = lambda a: jnp.concatenate([a] * SPLIT_PARTS + [jnp.zeros_like(a)]).reshape(1, LANES)
    const2 = lambda shape: pl.BlockSpec(shape, lambda i, c: (0, 0))
    state_in = pl.BlockSpec((None, None, STATE_ROWS, D_STATE), lambda i, c: (layer, i, 0, 0))
    state_out = pl.BlockSpec((None, STATE_ROWS, D_STATE), lambda i, c: (i, 0, 0))
    hist_spec = pl.BlockSpec((None, SUBLANES, CONV_DIM), lambda i, c: (i, 0, 0))
    to_x, to_seg = spread
    return pl.pallas_call(
        functools.partial(_ssd_kernel, valid),
        grid=(b, nc),
        in_specs=[pl.BlockSpec((None, qin, P_PAD), lambda i, c: (i, c, 0)),
                  hist_spec, state_in,
                  const2((CONV_W, CONV_DIM)), const2((1, CONV_DIM)), const2((1, LANES)),
                  const2((1, LANES)), const2((1, D_SSM)), const2((1, D_SSM)),
                  const2(to_x.shape), const2(to_seg.shape)],
        out_specs=[pl.BlockSpec((None, qin, D_SSM), lambda i, c: (i, c, 0)), hist_spec, state_out],
        out_shape=[jax.ShapeDtypeStruct((b, t, D_SSM), BF16),
                   jax.ShapeDtypeStruct((b, SUBLANES, CONV_DIM), F32),
                   jax.ShapeDtypeStruct((b, STATE_ROWS, D_STATE), F32)],
        scratch_shapes=[pltpu.VMEM((SUBLANES + CHUNK, CONV_DIM), F32), pltpu.VMEM((CHUNK, D_SSM), F32)],
        compiler_params=_cparams(2),
        name="ssd_mixer",
    )(proj, conv_init, h0, conv_w, conv_b.reshape(1, -1), copies(dt_bias), copies(a_log),
      jnp.repeat(d_skip, SSM_HEAD_DIM).reshape(1, -1), norm_w.reshape(1, -1), to_x, to_seg)


def _layer(xp, xs, mod_p, mod_s, weights, layer, attn_p, attn_s, conv_init_s, h0_s, spread, zeros_p, t_new):
    (norm1_w, w_in_t, conv_w, conv_b, dt_bias, a_log, d_skip, ssm_norm_w, w_out, norm2_w, w_gate, w_up,
     w_down) = weights
    ms = xs.shape[0]
    bs = ms // SAMPLE_ROWS
    as3 = lambda a: a[None]
    hp = norm_mod(xp, norm1_w[layer], mod_p[1], mod_p[0])
    hs = norm_mod(as3(xs), norm1_w[layer], as3(mod_s[1]), as3(mod_s[0]))[0]
    proj_p, proj_s = mm_inproj(hp, hs, w_in_t, layer)
    proj_s = proj_s.reshape(bs, SAMPLE_ROWS, P_PAD)

    ssd = functools.partial(ssd_mixer, conv_w=conv_w[layer], conv_b=conv_b[layer], dt_bias=dt_bias[layer],
                            a_log=a_log[layer], d_skip=d_skip[layer], norm_w=ssm_norm_w[layer], spread=spread)
    oa_p, k_p, v_p = attn_p(proj_p)
    os_p, tail_p, hn_p = ssd(proj_p, zeros_p[0], zeros_p[1], 0, qin=CHUNK, valid=CHUNK)
    oa_s, k_s, v_s = attn_s(proj_s)
    os_s, tail_s, hn_s = ssd(proj_s, conv_init_s, h0_s, layer, qin=SAMPLE_ROWS, valid=t_new)

    xp, xs = mm_outproj((oa_p, oa_s.reshape(ms, Q_W)), (os_p, os_s.reshape(ms, D_SSM)), w_out, layer,
                        (xp, xs), (mod_p[2], mod_s[2]))
    h2p = norm_mod(xp, norm2_w[layer], mod_p[4], mod_p[3])
    h2s = norm_mod(as3(xs), norm2_w[layer], as3(mod_s[4]), as3(mod_s[3]))[0]
    hid_p, hid_s, wdown_bf = mm_gateup(h2p, h2s, w_gate, w_up, w_down, layer)
    xp = mm_down(hid_p, wdown_bf, xp, mod_p[5])
    xs = mm_down(as3(hid_s), wdown_bf, as3(xs), as3(mod_s[5]))[0]

    keep = SUBLANES - (CONV_W - 1)
    state = lambda h: h.reshape(-1, N_SSM_HEADS, SSM_HEAD_DIM, D_STATE)
    heads = lambda a: a.reshape(a.shape[0], a.shape[1], N_KV, HEAD_DIM)
    return xp, xs, (heads(k_p), heads(v_p), tail_p[:, keep:], state(hn_p),
                    heads(k_s), heads(v_s), tail_s[:, keep:], state(hn_s))


def kernel(x_prompt, x_sample, cache_k, cache_v, state_conv, state_ssm, c_prompt, c_sample, w_ada, b_ada,
           norm1_w, w_in, conv_w, conv_b, dt_bias, a_log, d_skip, ssm_norm_w, sinks, w_out, norm2_w,
           w_gate, w_up, w_down, final_norm_w):
    depth = w_ada.shape[0]
    bp, tp, d = x_prompt.shape
    bs, ts, _ = x_sample.shape
    wc = cache_k.shape[2]
    assert ts <= SAMPLE_ROWS and tp % CHUNK == 0 and ts % CHUNK != 0

    c_all = jnp.concatenate([c_prompt, c_sample, jnp.zeros((ADA_ROWS - bp - bs, d), F32)], axis=0)
    mod_all = ada_mod(c_all, w_ada, b_ada)

    tab_p = rope_tables(jnp.arange(tp, dtype=jnp.int32))
    tab_s = rope_tables(PAST_LEN + jnp.arange(SAMPLE_ROWS, dtype=jnp.int32))
    spread = spread_matrices()
    weights = (norm1_w, jnp.swapaxes(w_in, 1, 2), conv_w, conv_b, dt_bias, a_log, d_skip, ssm_norm_w, w_out,
               norm2_w, w_gate, w_up, w_down)

    xp = x_prompt
    xs = jnp.pad(x_sample, ((0, 0), (0, SAMPLE_ROWS - ts), (0, 0))).reshape(bs * SAMPLE_ROWS, d)
    zeros_p = (jnp.zeros((bp, SUBLANES, CONV_DIM), F32), jnp.zeros((1, bp, STATE_ROWS, D_STATE), F32))
    h0_s = state_ssm.reshape(depth, bs, STATE_ROWS, D_STATE)
    ck = cache_k.reshape(depth, bs, wc, KV_W)
    cv = cache_v.reshape(depth, bs, wc, KV_W)

    outs = []
    for l in range(depth):
        mod_p = [mod_all[l, :bp, i * d:(i + 1) * d][:, None, :] for i in range(6)]
        mod_s = [jnp.repeat(mod_all[l, bp:bp + bs, i * d:(i + 1) * d], SAMPLE_ROWS, axis=0) for i in range(6)]
        attn_p = functools.partial(swa_prompt, sinks=sinks[l], tables=tab_p)
        attn_s = functools.partial(swa_decode, sinks=sinks[l], cache_k=ck, cache_v=cv, layer=l, tables=tab_s,
                                   t_new=ts)
        conv_init_s = jnp.pad(state_conv[l], ((0, 0), (SUBLANES - (CONV_W - 1), 0), (0, 0)))
        xp, xs, layer_outs = _layer(xp, xs, mod_p, mod_s, weights, l, attn_p, attn_s, conv_init_s, h0_s, spread,
                                    zeros_p, ts)
        outs.append(layer_outs)

    y_prompt = final_norm(xp, final_norm_w)
    y_sample = final_norm(xs[None], final_norm_w).reshape(bs, SAMPLE_ROWS, d)[:, :ts]
    return (y_prompt, y_sample) + tuple(jnp.stack(o) for o in zip(*outs))
```

```python
import functools

import jax
import jax.numpy as jnp
from jax import lax
from jax.experimental import pallas as pl
from jax.experimental.pallas import tpu as pltpu

F32 = jnp.float32
BF16 = jnp.bfloat16

D_MODEL = 4096
HEAD_DIM = 64
N_HEADS = 32
N_KV = 4
Q_PER_KV = N_HEADS // N_KV
WINDOW = 128
ROT_DIM = 16
ROPE_THETA = 500000.0
PAST_LEN = 16384
D_SSM = 2048
N_SSM_HEADS = 32
SSM_HEAD_DIM = 64
N_SSM_GROUPS = 4
SSM_R = N_SSM_HEADS // N_SSM_GROUPS
D_STATE = 128
CONV_W = 4
CONV_DIM = D_SSM + 2 * N_SSM_GROUPS * D_STATE
CHUNK = 128
D_FF = 11008
EPS = 1e-6
Q_W = N_HEADS * HEAD_DIM
KV_W = N_KV * HEAD_DIM
OFF_K = Q_W
OFF_V = Q_W + KV_W
OFF_Z = Q_W + 2 * KV_W
OFF_XBC = OFF_Z + D_SSM
OFF_DT = OFF_XBC + CONV_DIM
P_IN = OFF_DT + N_SSM_HEADS

LANES = 128
SUBLANES = 8
VMEM_LIMIT_BYTES = 56 * 1024 * 1024

P_PAD = -(-P_IN // LANES) * LANES
SAMPLE_ROWS = 8
ADA_ROWS = 40
SPLIT_PARTS = 3
STATE_ROWS = N_SSM_HEADS * SSM_HEAD_DIM


def _cparams(n_axes):
    return pltpu.CompilerParams(dimension_semantics=("arbitrary",) * n_axes, vmem_limit_bytes=VMEM_LIMIT_BYTES)


def _silu(x):
    return x * jax.nn.sigmoid(x)


def _dot(a, b):
    return jnp.dot(a, b, preferred_element_type=F32)


def _dot_nt(a, b):
    return lax.dot_general(a, b, (((1,), (1,)), ((), ())), preferred_element_type=F32)


def _dot_tn(a, b):
    return lax.dot_general(a, b, (((0,), (0,)), ((), ())), preferred_element_type=F32)


def _ada_kernel(c_ref, w_ref, b_ref, o_ref):
    x = _silu(c_ref[...]).astype(BF16)
    w = w_ref[...].astype(BF16)
    o_ref[...] = _dot(x, w) + b_ref[...]


def ada_mod(c_all, w_ada, b_ada, tn=1024):
    depth, d, n = w_ada.shape
    rows = c_all.shape[0]
    return pl.pallas_call(
        _ada_kernel,
        grid=(depth, n // tn),
        in_specs=[
            pl.BlockSpec((rows, d), lambda l, j: (0, 0)),
            pl.BlockSpec((None, d, tn), lambda l, j: (l, 0, j)),
            pl.BlockSpec((None, 1, tn), lambda l, j: (l, 0, j)),
        ],
        out_specs=pl.BlockSpec((None, rows, tn), lambda l, j: (l, 0, j)),
        out_shape=jax.ShapeDtypeStruct((depth, rows, n), F32),
        compiler_params=_cparams(2),
        name="ada_mod",
    )(c_all, w_ada, b_ada.reshape(depth, 1, n))


def _norm_mod_kernel(x_ref, nw_ref, sc_ref, sh_ref, o_ref):
    x = x_ref[...]
    y = x * lax.rsqrt(jnp.mean(x * x, axis=-1, keepdims=True) + EPS) * nw_ref[...]
    o_ref[...] = (y * (1.0 + sc_ref[...]) + sh_ref[...]).astype(o_ref.dtype)


def _norm_kernel(x_ref, nw_ref, o_ref):
    x = x_ref[...]
    o_ref[...] = x * lax.rsqrt(jnp.mean(x * x, axis=-1, keepdims=True) + EPS) * nw_ref[...]


def _row_tile(t, cap):
    tt = min(t, cap)
    assert t % tt == 0
    return tt


def norm_mod(x, nw, sc, sh, tt_cap=512):
    b, t, d = x.shape
    tt = _row_tile(t, tt_cap)
    mt = sc.shape[1]
    mod_spec = (pl.BlockSpec((None, 1, d), lambda i, j: (i, 0, 0)) if mt == 1
                else pl.BlockSpec((None, tt, d), lambda i, j: (i, j, 0)))
    return pl.pallas_call(
        _norm_mod_kernel,
        grid=(b, t // tt),
        in_specs=[pl.BlockSpec((None, tt, d), lambda i, j: (i, j, 0)),
                  pl.BlockSpec((1, d), lambda i, j: (0, 0)), mod_spec, mod_spec],
        out_specs=pl.BlockSpec((None, tt, d), lambda i, j: (i, j, 0)),
        out_shape=jax.ShapeDtypeStruct((b, t, d), BF16),
        compiler_params=_cparams(2),
        name="norm_mod",
    )(x, nw.reshape(1, d), sc, sh)


def final_norm(x, nw, tt_cap=256):
    b, t, d = x.shape
    tt = _row_tile(t, tt_cap)
    return pl.pallas_call(
        _norm_kernel,
        grid=(b, t // tt),
        in_specs=[pl.BlockSpec((None, tt, d), lambda i, j: (i, j, 0)),
                  pl.BlockSpec((1, d), lambda i, j: (0, 0))],
        out_specs=pl.BlockSpec((None, tt, d), lambda i, j: (i, j, 0)),
        out_shape=jax.ShapeDtypeStruct((b, t, d), F32),
        compiler_params=_cparams(2),
        name="final_norm",
    )(x, nw.reshape(1, d))


def _streams(b, t, tm):
    assert t % tm == 0
    per_b = t // tm
    steps = b * per_b

    def prompt_idx(s):
        return s // per_b, s % per_b

    return steps, prompt_idx


def _stream_specs(pidx, tm, ms, width, col=None):
    if col is None:
        return (pl.BlockSpec((None, tm, width), lambda j, s: (*pidx(s), 0)),
                pl.BlockSpec((ms, width), lambda j, s: (0, 0)))
    return (pl.BlockSpec((None, tm, col), lambda j, s: (*pidx(s), j)),
            pl.BlockSpec((ms, col), lambda j, s: (0, j)))


def _on_streams(prompt_fn, sample_fn):
    prompt_fn()
    pl.when(pl.program_id(1) == pl.num_programs(1) - 1)(sample_fn)


def _inproj_kernel(hp_ref, hs_ref, wt_ref, op_ref, os_ref, wbf):
    @pl.when(pl.program_id(1) == 0)
    def _():
        w = wt_ref[...]
        row = pl.program_id(0) * w.shape[0] + lax.broadcasted_iota(jnp.int32, w.shape, 0)
        wbf[...] = jnp.where(row < P_IN, w, 0.0).T.astype(BF16)

    def run(h_ref, o_ref):
        o_ref[...] = _dot(h_ref[...], wbf[...])

    _on_streams(functools.partial(run, hp_ref, op_ref), functools.partial(run, hs_ref, os_ref))


def mm_inproj(hp, hs, wt, layer, tm=1024, tn=512):
    b, t, k = hp.shape
    ms = hs.shape[0]
    tm = min(tm, t)
    steps, pidx = _streams(b, t, tm)
    return pl.pallas_call(
        _inproj_kernel,
        grid=(pl.cdiv(P_PAD, tn), steps),
        in_specs=[*_stream_specs(pidx, tm, ms, k),
                  pl.BlockSpec((None, tn, k), lambda j, s: (layer, j, 0))],
        out_specs=list(_stream_specs(pidx, tm, ms, P_PAD, tn)),
        out_shape=[jax.ShapeDtypeStruct((b, t, P_PAD), F32), jax.ShapeDtypeStruct((ms, P_PAD), F32)],
        scratch_shapes=[pltpu.VMEM((k, tn), BF16)],
        compiler_params=_cparams(2),
        name="mm_inproj",
    )(hp, hs, wt)


def _outproj_kernel(ap_ref, sp_ref, as_ref, ss_ref, w_ref, xp_ref, xs_ref, gp_ref, gs_ref, op_ref, os_ref, wbf):
    @pl.when(pl.program_id(1) == 0)
    def _():
        wbf[...] = w_ref[...].astype(BF16)

    def run(a_ref, s_ref, x_ref, g_ref, o_ref):
        ka = a_ref.shape[-1]
        acc = _dot(a_ref[...], wbf[:ka, :]) + _dot(s_ref[...], wbf[ka:, :])
        o_ref[...] = x_ref[...] + g_ref[...] * acc

    _on_streams(functools.partial(run, ap_ref, sp_ref, xp_ref, gp_ref, op_ref),
                functools.partial(run, as_ref, ss_ref, xs_ref, gs_ref, os_ref))


def mm_outproj(attn, ssm, w, layer, x, g, tm=1024, tn=512):
    b, t, ka = attn[0].shape
    ms = attn[1].shape[0]
    _, k, n = w.shape
    tm = min(tm, t)
    steps, pidx = _streams(b, t, tm)
    ap, as_ = _stream_specs(pidx, tm, ms, ka)
    sp, ss = _stream_specs(pidx, tm, ms, k - ka)
    xp, xs = _stream_specs(pidx, tm, ms, n, tn)
    gp = pl.BlockSpec((None, 1, tn), lambda j, s: (pidx(s)[0], 0, j))
    return pl.pallas_call(
        _outproj_kernel,
        grid=(n // tn, steps),
        in_specs=[ap, sp, as_, ss, pl.BlockSpec((None, k, tn), lambda j, s: (layer, 0, j)), xp, xs, gp, xs],
        out_specs=[xp, xs],
        out_shape=[jax.ShapeDtypeStruct((b, t, n), F32), jax.ShapeDtypeStruct((ms, n), F32)],
        scratch_shapes=[pltpu.VMEM((k, tn), BF16)],
        compiler_params=_cparams(2),
        name="mm_outproj",
    )(attn[0], ssm[0], attn[1], ssm[1], w, x[0], x[1], g[0], g[1])


def _gateup_kernel(hp_ref, hs_ref, wg_ref, wu_ref, wd_ref, op_ref, os_ref, wdbf_ref, wgbf, wubf):
    @pl.when(pl.program_id(1) == 0)
    def _():
        wgbf[...] = wg_ref[...].astype(BF16)
        wubf[...] = wu_ref[...].astype(BF16)

    wdbf_ref[...] = wd_ref[...].astype(BF16)

    def run(h_ref, o_ref):
        h = h_ref[...]
        o_ref[...] = (_silu(_dot(h, wgbf[...])) * _dot(h, wubf[...])).astype(o_ref.dtype)

    _on_streams(functools.partial(run, hp_ref, op_ref), functools.partial(run, hs_ref, os_ref))


def mm_gateup(hp, hs, wg, wu, wd, layer, tm=1024, tn=256):
    b, t, k = hp.shape
    ms = hs.shape[0]
    n = wg.shape[2]
    _, kd, nd = wd.shape
    tm = min(tm, t)
    steps, pidx = _streams(b, t, tm)
    n_tiles = n // tn
    rows = kd // (n_tiles * steps)
    assert n % tn == 0 and rows * n_tiles * steps == kd and rows % (2 * SUBLANES) == 0
    wd_row = lambda j, s: j * steps + s
    wspec = pl.BlockSpec((None, k, tn), lambda j, s: (layer, 0, j))
    return pl.pallas_call(
        _gateup_kernel,
        grid=(n_tiles, steps),
        in_specs=[*_stream_specs(pidx, tm, ms, k), wspec, wspec,
                  pl.BlockSpec((None, rows, nd), lambda j, s: (layer, wd_row(j, s), 0))],
        out_specs=[*_stream_specs(pidx, tm, ms, n, tn),
                   pl.BlockSpec((rows, nd), lambda j, s: (wd_row(j, s), 0))],
        out_shape=[jax.ShapeDtypeStruct((b, t, n), BF16), jax.ShapeDtypeStruct((ms, n), BF16),
                   jax.ShapeDtypeStruct((kd, nd), BF16)],
        scratch_shapes=[pltpu.VMEM((k, tn), BF16), pltpu.VMEM((k, tn), BF16)],
        compiler_params=_cparams(2),
        name="mm_gateup",
    )(hp, hs, wg, wu, wd)


def _down_kernel(h_ref, w_ref, x_ref, g_ref, o_ref):
    o_ref[...] = x_ref[...] + g_ref[...] * _dot(h_ref[...], w_ref[...])


def mm_down(hid, wbf, x, g, tm_cap=512, tn=512):
    b, t, k = hid.shape
    n = wbf.shape[1]
    tm = _row_tile(t, tm_cap)
    gspec = (pl.BlockSpec((None, 1, tn), lambda i, m, j: (i, 0, j)) if g.shape[1] == 1
             else pl.BlockSpec((None, tm, tn), lambda i, m, j: (i, m, j)))
    return pl.pallas_call(
        _down_kernel,
        grid=(b, t // tm, n // tn),
        in_specs=[pl.BlockSpec((None, tm, k), lambda i, m, j: (i, m, 0)),
                  pl.BlockSpec((k, tn), lambda i, m, j: (0, j)),
                  pl.BlockSpec((None, tm, tn), lambda i, m, j: (i, m, j)),
                  gspec],
        out_specs=pl.BlockSpec((None, tm, tn), lambda i, m, j: (i, m, j)),
        out_shape=jax.ShapeDtypeStruct((b, t, n), F32),
        compiler_params=_cparams(3),
        name="mm_down",
    )(hid, wbf, x, g)


def rope_tables(pos):
    half = ROT_DIM // 2
    inv = ROPE_THETA ** (-jnp.arange(0, ROT_DIM, 2, dtype=F32) / ROT_DIM)
    ang = pos.astype(F32)[:, None] * inv[None, :]
    cos, sin = jnp.cos(ang), jnp.sin(ang)
    t = pos.shape[0]
    rest = HEAD_DIM - ROT_DIM
    c = jnp.concatenate([cos, cos, jnp.ones((t, rest), F32)], axis=1)
    s_up = jnp.concatenate([-sin, jnp.zeros((t, half + rest), F32)], axis=1)
    s_dn = jnp.concatenate([jnp.zeros((t, half), F32), sin, jnp.zeros((t, rest), F32)], axis=1)
    reps = LANES // HEAD_DIM
    return tuple(jnp.tile(a, (1, reps)) for a in (c, s_up, s_dn))


def _rope(x, c, s_up, s_dn):
    w = x.shape[-1]
    reps = w // LANES
    half = ROT_DIM // 2
    tile = lambda a: jnp.concatenate([a] * reps, axis=1) if reps > 1 else a
    return (x * tile(c) + pltpu.roll(x, w - half, 1) * tile(s_up) + pltpu.roll(x, half, 1) * tile(s_dn))


def _low_head(shape):
    return lax.broadcasted_iota(jnp.int32, shape, len(shape) - 1) % LANES < HEAD_DIM


def _swa_prompt_kernel(sinks_ref, q_ref, kv_ref, c_ref, su_ref, sd_ref, o_ref, nk_ref, nv_ref, kprev, vprev):
    n = pl.program_id(1)
    last = pl.num_programs(1) - 1
    blk = q_ref.shape[0]

    @pl.when(n == 0)
    def _():
        kprev[...] = jnp.zeros_like(kprev)
        vprev[...] = jnp.zeros_like(vprev)

    c, su, sd = c_ref[...], su_ref[...], sd_ref[...]
    q = _rope(q_ref[...], c, su, sd) * (HEAD_DIM ** -0.5)
    low = _low_head(q.shape)
    q_heads = (jnp.where(low, q, 0.0).astype(BF16), jnp.where(low, 0.0, q).astype(BF16))
    kv = kv_ref[...]
    k = _rope(kv[:, :KV_W], c, su, sd)
    v = kv[:, KV_W:]

    @pl.when(n == last)
    def _():
        nk_ref[...] = k
        nv_ref[...] = v

    kk = jnp.concatenate([kprev[(n + 1) % 2], k], axis=0)
    vv = jnp.concatenate([vprev[(n + 1) % 2], v], axis=0)
    low2 = _low_head((2 * blk, LANES))
    low1 = _low_head((blk, LANES))
    row = lax.broadcasted_iota(jnp.int32, (blk, blk), 0)
    col = lax.broadcasted_iota(jnp.int32, (blk, blk), 1)
    from_prev = col > row
    no_prev = jnp.where(n > 0, 0.0, -jnp.inf)

    def both_halves(x, g):
        tile = x[:, (g // 2) * LANES:(g // 2 + 1) * LANES]
        other = pltpu.roll(tile, HEAD_DIM, 1)
        return (jnp.where(low2, other, tile) if g % 2 else jnp.where(low2, tile, other)).astype(BF16)

    for g in range(N_KV):
        kd, vd = both_halves(kk, g), both_halves(vv, g)
        for jp in range(Q_PER_KV // 2):
            lanes = slice((g * Q_PER_KV // 2 + jp) * LANES, (g * Q_PER_KV // 2 + jp + 1) * LANES)
            outs = []
            for half in range(2):
                h = g * Q_PER_KV + 2 * jp + half
                s = _dot_nt(q_heads[half][:, lanes], kd)
                s = jnp.where(from_prev, s[:, :blk] + no_prev, s[:, blk:])
                sink = sinks_ref[h]
                m = jnp.maximum(jnp.max(s, axis=-1, keepdims=True), sink)
                e = jnp.exp(s - m)
                den = jnp.sum(e, axis=-1, keepdims=True) + jnp.exp(sink - m)
                pe = jnp.concatenate([jnp.where(from_prev, e, 0.0), jnp.where(from_prev, 0.0, e)], axis=1)
                outs.append(_dot(pe.astype(BF16), vd) * (1.0 / den))
            o_ref[:, lanes] = jnp.where(low1, outs[0], outs[1]).astype(o_ref.dtype)
    kprev[n % 2] = k
    vprev[n % 2] = v


def swa_prompt(proj, sinks, tables):
    b, t, _ = proj.shape
    blk = WINDOW
    nb = t // blk
    kv_col = OFF_K // (2 * KV_W)
    tab_spec = pl.BlockSpec((blk, LANES), lambda i, n: (n, 0))
    keep_spec = pl.BlockSpec((None, blk, KV_W), lambda i, n: (i, 0, 0))
    return pl.pallas_call(
        _swa_prompt_kernel,
        grid=(b, nb),
        in_specs=[pl.BlockSpec(memory_space=pltpu.SMEM),
                  pl.BlockSpec((None, blk, Q_W), lambda i, n: (i, n, 0)),
                  pl.BlockSpec((None, blk, 2 * KV_W), lambda i, n: (i, n, kv_col)),
                  tab_spec, tab_spec, tab_spec],
        out_specs=[pl.BlockSpec((None, blk, Q_W), lambda i, n: (i, n, 0)), keep_spec, keep_spec],
        out_shape=[jax.ShapeDtypeStruct((b, t, Q_W), BF16),
                   jax.ShapeDtypeStruct((b, blk, KV_W), F32),
                   jax.ShapeDtypeStruct((b, blk, KV_W), F32)],
        scratch_shapes=[pltpu.VMEM((2, blk, KV_W), F32), pltpu.VMEM((2, blk, KV_W), F32)],
        compiler_params=_cparams(2),
        name="swa_prompt",
    )(sinks, proj, proj, *tables)


def _swa_decode_kernel(t_new, sinks_ref, q_ref, kv_ref, kc_ref, vc_ref, c_ref, su_ref, sd_ref,
                       o_ref, nk_ref, nv_ref, kk, vv):
    rows = q_ref.shape[0]
    wc = kc_ref.shape[0]
    c, su, sd = c_ref[...], su_ref[...], sd_ref[...]
    q = (_rope(q_ref[...], c, su, sd) * (HEAD_DIM ** -0.5)).astype(BF16)
    kv = kv_ref[...]
    k = _rope(kv[:, :KV_W], c, su, sd)
    v = kv[:, KV_W:]
    kc, vc = kc_ref[...], vc_ref[...]

    nk_ref[0:wc - t_new, :] = kc[t_new:, :]
    nv_ref[0:wc - t_new, :] = vc[t_new:, :]
    nk_ref[wc - t_new:wc, :] = k[0:t_new, :]
    nv_ref[wc - t_new:wc, :] = v[0:t_new, :]

    kk[...] = jnp.zeros_like(kk)
    vv[...] = jnp.zeros_like(vv)
    kk[0:wc, :] = kc.astype(BF16)
    vv[0:wc, :] = vc.astype(BF16)
    kk[wc:wc + rows, :] = k.astype(BF16)
    vv[wc:wc + rows, :] = v.astype(BF16)

    nkeys = kk.shape[0]
    nrow = Q_PER_KV * rows
    r = lax.broadcasted_iota(jnp.int32, (nrow, nkeys), 0)
    sj = lax.broadcasted_iota(jnp.int32, (nrow, nkeys), 1)
    tq = r % rows
    mask = jnp.logical_and(jnp.logical_and(sj <= tq + wc, sj > tq + wc - WINDOW), sj < wc + t_new)
    rcol = lax.broadcasted_iota(jnp.int32, (nrow, 1), 0) // rows
    for g in range(N_KV):
        kg = kk[:, g * HEAD_DIM:(g + 1) * HEAD_DIM]
        vg = vv[:, g * HEAD_DIM:(g + 1) * HEAD_DIM]
        qs = jnp.concatenate([q[:, (g * Q_PER_KV + hh) * HEAD_DIM:(g * Q_PER_KV + hh + 1) * HEAD_DIM]
                              for hh in range(Q_PER_KV)], axis=0)
        sink = jnp.zeros((nrow, 1), F32)
        for hh in range(Q_PER_KV):
            sink = jnp.where(rcol == hh, sinks_ref[g * Q_PER_KV + hh], sink)
        s = jnp.where(mask, _dot_nt(qs, kg), -jnp.inf)
        m = jnp.maximum(jnp.max(s, axis=-1, keepdims=True), sink)
        e = jnp.exp(s - m)
        p = e / (jnp.sum(e, axis=-1, keepdims=True) + jnp.exp(sink - m))
        og = _dot(p.astype(BF16), vg)
        for hh in range(Q_PER_KV):
            h = g * Q_PER_KV + hh
            o_ref[:, h * HEAD_DIM:(h + 1) * HEAD_DIM] = og[hh * rows:(hh + 1) * rows, :].astype(o_ref.dtype)


def swa_decode(proj, sinks, cache_k, cache_v, layer, tables, t_new):
    b, rows, _ = proj.shape
    wc = cache_k.shape[2]
    kv_col = OFF_K // (2 * KV_W)
    tab_spec = pl.BlockSpec((rows, LANES), lambda i: (0, 0))
    cache_in = pl.BlockSpec((None, None, wc, KV_W), lambda i: (layer, i, 0, 0))
    cache_out = pl.BlockSpec((None, wc, KV_W), lambda i: (i, 0, 0))
    nkeys = 2 * WINDOW
    return pl.pallas_call(
        functools.partial(_swa_decode_kernel, t_new),
        grid=(b,),
        in_specs=[pl.BlockSpec(memory_space=pltpu.SMEM),
                  pl.BlockSpec((None, rows, Q_W), lambda i: (i, 0, 0)),
                  pl.BlockSpec((None, rows, 2 * KV_W), lambda i: (i, 0, kv_col)),
                  cache_in, cache_in, tab_spec, tab_spec, tab_spec],
        out_specs=[pl.BlockSpec((None, rows, Q_W), lambda i: (i, 0, 0)), cache_out, cache_out],
        out_shape=[jax.ShapeDtypeStruct((b, rows, Q_W), BF16),
                   jax.ShapeDtypeStruct((b, wc, KV_W), F32),
                   jax.ShapeDtypeStruct((b, wc, KV_W), F32)],
        scratch_shapes=[pltpu.VMEM((nkeys, KV_W), BF16), pltpu.VMEM((nkeys, KV_W), BF16)],
        compiler_params=_cparams(1),
        name="swa_decode",
    )(sinks, proj, proj, cache_k, cache_v, *tables)


def spread_matrices():
    src = jnp.arange(LANES)
    head = src % N_SSM_HEADS
    live = src < SPLIT_PARTS * N_SSM_HEADS
    to_x = (head[:, None] == (jnp.arange(D_SSM) // SSM_HEAD_DIM)[None, :]) & live[:, None]
    to_seg = (head[:, None] == (jnp.arange(N_SSM_HEADS * CHUNK) // CHUNK)[None, :]) & live[:, None]
    return to_x.astype(BF16), to_seg.astype(BF16)


def _three_copies(x):
    return x + pltpu.roll(x, N_SSM_HEADS, 1) + pltpu.roll(x, 2 * N_SSM_HEADS, 1)


def _split_parts(x3, part):
    hi = x3.astype(BF16).astype(F32)
    r1 = x3 - hi
    mid = r1.astype(BF16).astype(F32)
    lo = r1 - mid
    return jnp.where(part == 0, hi, jnp.where(part == 1, mid, jnp.where(part == 2, lo, 0.0))).astype(BF16)


def _ssd_kernel(valid, proj_ref, cinit_ref, h0_ref, cw_ref, cb_ref, dtb_ref, alog_ref, dsk_ref, nw_ref,
                tox_ref, toseg_ref, o_ref, tail_ref, hn_ref, xp, ybuf):
    cidx = pl.program_id(1)
    qin = proj_ref.shape[0]
    q = ybuf.shape[0]
    pad = q - qin
    hist = SUBLANES

    @pl.when(cidx == 0)
    def _():
        xp[0:hist, :] = cinit_ref[...]
        hn_ref[...] = h0_ref[...]

    blk = proj_ref[...]
    xp[hist:hist + qin, :] = blk[:, OFF_XBC:OFF_DT]
    if pad:
        xp[hist + qin:, :] = jnp.zeros((pad, CONV_DIM), F32)
    tail_ref[...] = xp[valid:valid + hist, :]

    cw = cw_ref[...]
    conv = cb_ref[...] + cw[CONV_W - 1:CONV_W, :] * xp[hist:hist + q, :]
    for j in range(CONV_W - 1):
        shift = CONV_W - 1 - j
        conv = conv + cw[j:j + 1, :] * xp[hist - shift:hist - shift + q, :]
    xc = _silu(conv)
    xp[0:hist, :] = xp[q:q + hist, :]

    xs = xc[:, :D_SSM]
    gw = D_STATE
    bmat = xc[:, D_SSM:D_SSM + N_SSM_GROUPS * gw].astype(BF16)
    cmat = xc[:, D_SSM + N_SSM_GROUPS * gw:].astype(BF16)

    lane = lax.broadcasted_iota(jnp.int32, (q, LANES), 1)
    rowi = lax.broadcasted_iota(jnp.int32, (q, LANES), 0)
    part = lane // N_SSM_HEADS
    dt_raw = blk[:, OFF_DT:OFF_DT + LANES]
    if pad:
        dt_raw = jnp.concatenate([dt_raw, jnp.zeros((pad, LANES), F32)], axis=0)
    dt_raw = _three_copies(jnp.where(lane < N_SSM_HEADS, dt_raw, 0.0))
    xdtb = dt_raw + dtb_ref[...]
    dt = jnp.maximum(xdtb, 0.0) + jnp.log1p(jnp.exp(-jnp.abs(xdtb)))
    dt = jnp.where(jnp.logical_and(rowi < valid, part < SPLIT_PARTS), dt, 0.0)
    a = dt * (-jnp.exp(alog_ref[...]))

    li = lax.broadcasted_iota(jnp.int32, (q, q), 0)
    si = lax.broadcasted_iota(jnp.int32, (q, q), 1)
    causal = li >= si
    tri = causal.astype(BF16)
    triu = (li <= si).astype(BF16)
    pa = _split_parts(a, part)
    c3 = _dot(tri, pa)
    acum = c3
    for k in range(1, LANES // N_SSM_HEADS):
        acum = acum + pltpu.roll(c3, k * N_SSM_HEADS, 1)
    ct = _dot_tn(pa, triu)
    acum_t = ct[0:N_SSM_HEADS] + ct[N_SSM_HEADS:2 * N_SSM_HEADS] + ct[2 * N_SSM_HEADS:3 * N_SSM_HEADS]
    alast = acum[q - 1:q, :]
    e_acum = jnp.exp(acum)
    e_chunk = e_acum[q - 1:q, :]

    tox = tox_ref[...]
    dtx = _dot(_split_parts(dt, part), tox)
    e_end_x = _dot(_split_parts(jnp.exp(alast - acum), part), tox)
    e_acum_x = _dot(_split_parts(e_acum, part), tox)
    pacum = _split_parts(acum, part)

    xdt = xs * dtx
    low = _low_head(xdt.shape)
    xdt_heads = (jnp.where(low, xdt, 0.0).astype(BF16), jnp.where(low, 0.0, xdt).astype(BF16))
    xe = (xdt * e_end_x).astype(BF16)
    ybuf[...] = dsk_ref[...] * xs

    gsz = SSM_R * SSM_HEAD_DIM
    for g in range(N_SSM_GROUPS):
        bg = bmat[:, g * gw:(g + 1) * gw]
        cg = cmat[:, g * gw:(g + 1) * gw]
        glanes = slice(g * gsz, (g + 1) * gsz)
        grows = slice(g * gsz, (g + 1) * gsz)
        cb = _dot_nt(cg, bg)
        hold = hn_ref[grows, :]
        ybuf[:, glanes] += _dot_nt(cg, hold.astype(BF16)) * e_acum_x[:, glanes]
        seg = _dot(pacum, toseg_ref[:, g * SSM_R * q:(g + 1) * SSM_R * q])
        for jp in range(SSM_R // 2):
            ms = []
            for half in range(2):
                r = 2 * jp + half
                h = g * SSM_R + r
                d = seg[:, r * q:(r + 1) * q] - acum_t[h:h + 1, :]
                ms.append((cb * jnp.exp(jnp.where(causal, d, -jnp.inf))).astype(BF16))
            plane = slice(g * gsz + jp * LANES, g * gsz + (jp + 1) * LANES)
            rhs = jnp.concatenate([xdt_heads[0][:, plane], xdt_heads[1][:, plane]], axis=0)
            ybuf[:, plane] += _dot(jnp.concatenate(ms, axis=1), rhs)
        st = _dot_tn(xe[:, glanes], bg)
        for r in range(SSM_R):
            h = g * SSM_R + r
            rows = slice(h * SSM_HEAD_DIM, (h + 1) * SSM_HEAD_DIM)
            hn_ref[rows, :] = (hold[r * SSM_HEAD_DIM:(r + 1) * SSM_HEAD_DIM, :] * e_chunk[:, h:h + 1]
                               + st[r * SSM_HEAD_DIM:(r + 1) * SSM_HEAD_DIM, :])

    z = blk[:, OFF_Z:OFF_XBC]
    gated = ybuf[0:qin, :] * _silu(z)
    nsz = D_SSM // N_SSM_GROUPS
    nw = nw_ref[...]
    for g in range(N_SSM_GROUPS):
        gg = gated[:, g * nsz:(g + 1) * nsz]
        gg = gg * lax.rsqrt(jnp.mean(gg * gg, axis=-1, keepdims=True) + EPS)
        o_ref[:, g * nsz:(g + 1) * nsz] = (gg * nw[:, g * nsz:(g + 1) * nsz]).astype(o_ref.dtype)


def ssd_mixer(proj, conv_init, h0, layer, conv_w, conv_b, dt_bias, a_log, d_skip, norm_w, spread, qin, valid):
    b, t, _ = proj.shape
    nc = t // qin
    copies = lambda a: jnp.concatenate([a] * SPLIT_PARTS + [jnp.zeros_like(a)]).reshape(1, LANES)
    const2 = lambda shape: pl.BlockSpec(shape, lambda i, c: (0, 0))
    state_in = pl.BlockSpec((None, None, STATE_ROWS, D_STATE), lambda i, c: (layer, i, 0, 0))
    state_out = pl.BlockSpec((None, STATE_ROWS, D_STATE), lambda i, c: (i, 0, 0))
    hist_spec = pl.BlockSpec((None, SUBLANES, CONV_DIM), lambda i, c: (i, 0, 0))
    to_x, to_seg = spread
    return pl.pallas_call(
        functools.partial(_ssd_kernel, valid),
        grid=(b, nc),
        in_specs=[pl.BlockSpec((None, qin, P_PAD), lambda i, c: (i, c, 0)),
                  hist_spec, state_in,
                  const2((CONV_W, CONV_DIM)), const2((1, CONV_DIM)), const2((1, LANES)),
                  const2((1, LANES)), const2((1, D_SSM)), const2((1, D_SSM)),
                  const2(to_x.shape), const2(to_seg.shape)],
        out_specs=[pl.BlockSpec((None, qin, D_SSM), lambda i, c: (i, c, 0)), hist_spec, state_out],
        out_shape=[jax.ShapeDtypeStruct((b, t, D_SSM), BF16),
                   jax.ShapeDtypeStruct((b, SUBLANES, CONV_DIM), F32),
                   jax.ShapeDtypeStruct((b, STATE_ROWS, D_STATE), F32)],
        scratch_shapes=[pltpu.VMEM((SUBLANES + CHUNK, CONV_DIM), F32), pltpu.VMEM((CHUNK, D_SSM), F32)],
        compiler_params=_cparams(2),
        name="ssd_mixer",
    )(proj, conv_init, h0, conv_w, conv_b.reshape(1, -1), copies(dt_bias), copies(a_log),
      jnp.repeat(d_skip, SSM_HEAD_DIM).reshape(1, -1), norm_w.reshape(1, -1), to_x, to_seg)


def _layer(xp, xs, mod_p, mod_s, weights, layer, attn_p, attn_s, conv_init_s, h0_s, spread, zeros_p, t_new):
    (norm1_w, w_in_t, conv_w, conv_b, dt_bias, a_log, d_skip, ssm_norm_w, w_out, norm2_w, w_gate, w_up,
     w_down) = weights
    ms = xs.shape[0]
    bs = ms // SAMPLE_ROWS
    as3 = lambda a: a[None]
    hp = norm_mod(xp, norm1_w[layer], mod_p[1], mod_p[0])
    hs = norm_mod(as3(xs), norm1_w[layer], as3(mod_s[1]), as3(mod_s[0]))[0]
    proj_p, proj_s = mm_inproj(hp, hs, w_in_t, layer)
    proj_s = proj_s.reshape(bs, SAMPLE_ROWS, P_PAD)

    ssd = functools.partial(ssd_mixer, conv_w=conv_w[layer], conv_b=conv_b[layer], dt_bias=dt_bias[layer],
                            a_log=a_log[layer], d_skip=d_skip[layer], norm_w=ssm_norm_w[layer], spread=spread)
    oa_p, k_p, v_p = attn_p(proj_p)
    os_p, tail_p, hn_p = ssd(proj_p, zeros_p[0], zeros_p[1], 0, qin=CHUNK, valid=CHUNK)
    oa_s, k_s, v_s = attn_s(proj_s)
    os_s, tail_s, hn_s = ssd(proj_s, conv_init_s, h0_s, layer, qin=SAMPLE_ROWS, valid=t_new)

    xp, xs = mm_outproj((oa_p, oa_s.reshape(ms, Q_W)), (os_p, os_s.reshape(ms, D_SSM)), w_out, layer,
                        (xp, xs), (mod_p[2], mod_s[2]))
    h2p = norm_mod(xp, norm2_w[layer], mod_p[4], mod_p[3])
    h2s = norm_mod(as3(xs), norm2_w[layer], as3(mod_s[4]), as3(mod_s[3]))[0]
    hid_p, hid_s, wdown_bf = mm_gateup(h2p, h2s, w_gate, w_up, w_down, layer)
    xp = mm_down(hid_p, wdown_bf, xp, mod_p[5])
    xs = mm_down(as3(hid_s), wdown_bf, as3(xs), as3(mod_s[5]))[0]

    keep = SUBLANES - (CONV_W - 1)
    state = lambda h: h.reshape(-1, N_SSM_HEADS, SSM_HEAD_DIM, D_STATE)
    heads = lambda a: a.reshape(a.shape[0], a.shape[1], N_KV, HEAD_DIM)
    return xp, xs, (heads(k_p), heads(v_p), tail_p[:, keep:], state(hn_p),
                    heads(k_s), heads(v_s), tail_s[:, keep:], state(hn_s))


def kernel(x_prompt, x_sample, cache_k, cache_v, state_conv, state_ssm, c_prompt, c_sample, w_ada, b_ada,
           norm1_w, w_in, conv_w, conv_b, dt_bias, a_log, d_skip, ssm_norm_w, sinks, w_out, norm2_w,
           w_gate, w_up, w_down, final_norm_w):
    depth = w_ada.shape[0]
    bp, tp, d = x_prompt.shape
    bs, ts, _ = x_sample.shape
    wc = cache_k.shape[2]
    assert ts <= SAMPLE_ROWS and tp % CHUNK == 0 and ts % CHUNK != 0

    c_all = jnp.concatenate([c_prompt, c_sample, jnp.zeros((ADA_ROWS - bp - bs, d), F32)], axis=0)
    mod_all = ada_mod(c_all, w_ada, b_ada)

    tab_p = rope_tables(jnp.arange(tp, dtype=jnp.int32))
    tab_s = rope_tables(PAST_LEN + jnp.arange(SAMPLE_ROWS, dtype=jnp.int32))
    spread = spread_matrices()
    weights = (norm1_w, jnp.swapaxes(w_in, 1, 2), conv_w, conv_b, dt_bias, a_log, d_skip, ssm_norm_w, w_out,
               norm2_w, w_gate, w_up, w_down)

    xp = x_prompt
    xs = jnp.pad(x_sample, ((0, 0), (0, SAMPLE_ROWS - ts), (0, 0))).reshape(bs * SAMPLE_ROWS, d)
    zeros_p = (jnp.zeros((bp, SUBLANES, CONV_DIM), F32), jnp.zeros((1, bp, STATE_ROWS, D_STATE), F32))
    h0_s = state_ssm.reshape(depth, bs, STATE_ROWS, D_STATE)
    ck = cache_k.reshape(depth, bs, wc, KV_W)
    cv = cache_v.reshape(depth, bs, wc, KV_W)

    outs = []
    for l in range(depth):
        mod_p = [mod_all[l, :bp, i * d:(i + 1) * d][:, None, :] for i in range(6)]
        mod_s = [jnp.repeat(mod_all[l, bp:bp + bs, i * d:(i + 1) * d], SAMPLE_ROWS, axis=0) for i in range(6)]
        attn_p = functools.partial(swa_prompt, sinks=sinks[l], tables=tab_p)
        attn_s = functools.partial(swa_decode, sinks=sinks[l], cache_k=ck, cache_v=cv, layer=l, tables=tab_s,
                                   t_new=ts)
        conv_init_s = jnp.pad(state_conv[l], ((0, 0), (SUBLANES - (CONV_W - 1), 0), (0, 0)))
        xp, xs, layer_outs = _layer(xp, xs, mod_p, mod_s, weights, l, attn_p, attn_s, conv_init_s, h0_s, spread,
                                    zeros_p, ts)
        outs.append(layer_outs)

    y_prompt = final_norm(xp, final_norm_w)
    y_sample = final_norm(xs[None], final_norm_w).reshape(bs, SAMPLE_ROWS, d)[:, :ts]
    return (y_prompt, y_sample) + tuple(jnp.stack(o) for o in zip(*outs))
```

```python
import functools

import jax
import jax.numpy as jnp
from jax import lax
from jax.experimental import pallas as pl
from jax.experimental.pallas import tpu as pltpu

F32 = jnp.float32
BF16 = jnp.bfloat16

D_MODEL = 4096
HEAD_DIM = 64
N_HEADS = 32
N_KV = 4
Q_PER_KV = N_HEADS // N_KV
WINDOW = 128
ROT_DIM = 16
ROPE_THETA = 500000.0
PAST_LEN = 16384
D_SSM = 2048
N_SSM_HEADS = 32
SSM_HEAD_DIM = 64
N_SSM_GROUPS = 4
SSM_R = N_SSM_HEADS // N_SSM_GROUPS
D_STATE = 128
CONV_W = 4
CONV_DIM = D_SSM + 2 * N_SSM_GROUPS * D_STATE
CHUNK = 128
D_FF = 11008
EPS = 1e-6
Q_W = N_HEADS * HEAD_DIM
KV_W = N_KV * HEAD_DIM
OFF_K = Q_W
OFF_V = Q_W + KV_W
OFF_Z = Q_W + 2 * KV_W
OFF_XBC = OFF_Z + D_SSM
OFF_DT = OFF_XBC + CONV_DIM
P_IN = OFF_DT + N_SSM_HEADS

LANES = 128
SUBLANES = 8
MXU_WIDTH = 256
CAST_ROWS = 512
VMEM_LIMIT_BYTES = 56 * 1024 * 1024

P_PAD = -(-P_IN // LANES) * LANES
SAMPLE_ROWS = 8
ADA_ROWS = 40
SPLIT_PARTS = 3
STATE_ROWS = N_SSM_HEADS * SSM_HEAD_DIM


def _cparams(n_axes):
    return pltpu.CompilerParams(dimension_semantics=("arbitrary",) * n_axes, vmem_limit_bytes=VMEM_LIMIT_BYTES)


def _silu(x):
    return x * jax.nn.sigmoid(x)


def _dot(a, b):
    return jnp.dot(a, b, preferred_element_type=F32)


def _dot_nt(a, b):
    return lax.dot_general(a, b, (((1,), (1,)), ((), ())), preferred_element_type=F32)


def _dot_tn(a, b):
    return lax.dot_general(a, b, (((0,), (0,)), ((), ())), preferred_element_type=F32)


def _ada_kernel(c_ref, w_ref, b_ref, o_ref):
    x = _silu(c_ref[...]).astype(BF16)
    w = w_ref[...].astype(BF16)
    o_ref[...] = _dot(x, w) + b_ref[...]


def ada_mod(c_all, w_ada, b_ada, tn=1024):
    depth, d, n = w_ada.shape
    rows = c_all.shape[0]
    return pl.pallas_call(
        _ada_kernel,
        grid=(depth, n // tn),
        in_specs=[
            pl.BlockSpec((rows, d), lambda l, j: (0, 0)),
            pl.BlockSpec((None, d, tn), lambda l, j: (l, 0, j)),
            pl.BlockSpec((None, 1, tn), lambda l, j: (l, 0, j)),
        ],
        out_specs=pl.BlockSpec((None, rows, tn), lambda l, j: (l, 0, j)),
        out_shape=jax.ShapeDtypeStruct((depth, rows, n), F32),
        compiler_params=_cparams(2),
        name="ada_mod",
    )(c_all, w_ada, b_ada.reshape(depth, 1, n))


def _norm_mod_kernel(x_ref, nw_ref, sc_ref, sh_ref, o_ref):
    x = x_ref[...]
    y = x * lax.rsqrt(jnp.mean(x * x, axis=-1, keepdims=True) + EPS) * nw_ref[...]
    o_ref[...] = (y * (1.0 + sc_ref[...]) + sh_ref[...]).astype(o_ref.dtype)


def _norm_kernel(x_ref, nw_ref, o_ref):
    x = x_ref[...]
    o_ref[...] = x * lax.rsqrt(jnp.mean(x * x, axis=-1, keepdims=True) + EPS) * nw_ref[...]


def _row_tile(t, cap):
    tt = min(t, cap)
    assert t % tt == 0
    return tt


def norm_mod(x, nw, sc, sh, tt_cap=512):
    b, t, d = x.shape
    tt = _row_tile(t, tt_cap)
    mt = sc.shape[1]
    mod_spec = (pl.BlockSpec((None, 1, d), lambda i, j: (i, 0, 0)) if mt == 1
                else pl.BlockSpec((None, tt, d), lambda i, j: (i, j, 0)))
    return pl.pallas_call(
        _norm_mod_kernel,
        grid=(b, t // tt),
        in_specs=[pl.BlockSpec((None, tt, d), lambda i, j: (i, j, 0)),
                  pl.BlockSpec((1, d), lambda i, j: (0, 0)), mod_spec, mod_spec],
        out_specs=pl.BlockSpec((None, tt, d), lambda i, j: (i, j, 0)),
        out_shape=jax.ShapeDtypeStruct((b, t, d), BF16),
        compiler_params=_cparams(2),
        name="norm_mod",
    )(x, nw.reshape(1, d), sc, sh)


def final_norm(x, nw, tt_cap=256):
    b, t, d = x.shape
    tt = _row_tile(t, tt_cap)
    return pl.pallas_call(
        _norm_kernel,
        grid=(b, t // tt),
        in_specs=[pl.BlockSpec((None, tt, d), lambda i, j: (i, j, 0)),
                  pl.BlockSpec((1, d), lambda i, j: (0, 0))],
        out_specs=pl.BlockSpec((None, tt, d), lambda i, j: (i, j, 0)),
        out_shape=jax.ShapeDtypeStruct((b, t, d), F32),
        compiler_params=_cparams(2),
        name="final_norm",
    )(x, nw.reshape(1, d))


def _streams(b, t, tm):
    assert t % tm == 0
    per_b = t // tm
    steps = b * per_b

    def prompt_idx(s):
        return s // per_b, s % per_b

    return steps, prompt_idx


def _stream_specs(pidx, tm, ms, width, col=None):
    if col is None:
        return (pl.BlockSpec((None, tm, width), lambda j, s: (*pidx(s), 0)),
                pl.BlockSpec((ms, width), lambda j, s: (0, 0)))
    return (pl.BlockSpec((None, tm, col), lambda j, s: (*pidx(s), j)),
            pl.BlockSpec((ms, col), lambda j, s: (0, j)))


def _on_streams(prompt_fn, sample_fn):
    prompt_fn()
    pl.when(pl.program_id(1) == pl.num_programs(1) - 1)(sample_fn)


def _staged_blocks(n_tiles, tail, full, copies):
    def for_block(jj, action):
        def run(block, extent):
            for c in copies(block, extent):
                action(c)

        if tail == full:
            run(jj, full)
        else:
            pl.when(jj < n_tiles - 1)(functools.partial(run, jj, full))
            pl.when(jj == n_tiles - 1)(functools.partial(run, n_tiles - 1, tail))

    return (functools.partial(for_block, action=lambda c: c.start()),
            functools.partial(for_block, action=lambda c: c.wait()))


def _inproj_kernel(layer, n_tiles, tail, hp_ref, hs_ref, wt_hbm, op_ref, os_ref, stage, wbf, sems):
    j = pl.program_id(0)
    tn = stage.shape[0]

    def copies(jj, rows):
        return [pltpu.make_async_copy(wt_hbm.at[layer, pl.ds(jj * tn, rows), :], stage.at[pl.ds(0, rows), :],
                                      sems.at[0])]

    start, wait = _staged_blocks(n_tiles, tail, tn, copies)

    @pl.when(pl.program_id(1) == 0)
    def _():
        pl.when(j == 0)(functools.partial(start, j))
        wait(j)
        for c in range(tn // MXU_WIDTH):
            w = stage[c * MXU_WIDTH:(c + 1) * MXU_WIDTH, :]
            row = j * tn + c * MXU_WIDTH + lax.broadcasted_iota(jnp.int32, w.shape, 0)
            wbf[:, c * MXU_WIDTH:(c + 1) * MXU_WIDTH] = jnp.where(row < P_IN, w, 0.0).T.astype(BF16)
        pl.when(j + 1 < n_tiles)(functools.partial(start, j + 1))

    def run(h_ref, o_ref):
        for c in range(tn // MXU_WIDTH):
            cols = slice(c * MXU_WIDTH, (c + 1) * MXU_WIDTH)

            def one_width(cols=cols):
                o_ref[:, cols] = _dot(h_ref[...], wbf[:, cols])

            pl.when(j >= 0)(one_width)

    _on_streams(functools.partial(run, hp_ref, op_ref), functools.partial(run, hs_ref, os_ref))


def mm_inproj(hp, hs, wt, layer, tm=1024, tn=1024):
    b, t, k = hp.shape
    ms = hs.shape[0]
    tm = min(tm, t)
    steps, pidx = _streams(b, t, tm)
    n_tiles = pl.cdiv(P_PAD, tn)
    tail = P_IN - (n_tiles - 1) * tn
    assert n_tiles >= 2 and tail > 0 and tail % SUBLANES == 0 and tn % MXU_WIDTH == 0
    return pl.pallas_call(
        functools.partial(_inproj_kernel, layer, n_tiles, tail),
        grid=(n_tiles, steps),
        in_specs=[*_stream_specs(pidx, tm, ms, k), pl.BlockSpec(memory_space=pltpu.HBM)],
        out_specs=list(_stream_specs(pidx, tm, ms, P_PAD, tn)),
        out_shape=[jax.ShapeDtypeStruct((b, t, P_PAD), F32), jax.ShapeDtypeStruct((ms, P_PAD), F32)],
        scratch_shapes=[pltpu.VMEM((tn, k), F32), pltpu.VMEM((k, tn), BF16), pltpu.SemaphoreType.DMA((1,))],
        compiler_params=_cparams(2),
        name="mm_inproj",
    )(hp, hs, wt)


def _outproj_kernel(ap_ref, sp_ref, as_ref, ss_ref, w_ref, xp_ref, xs_ref, gp_ref, gs_ref, op_ref, os_ref, wbf):
    @pl.when(pl.program_id(1) == 0)
    def _():
        wbf[...] = w_ref[...].astype(BF16)

    def run(a_ref, s_ref, x_ref, g_ref, o_ref):
        ka = a_ref.shape[-1]
        acc = _dot(a_ref[...], wbf[:ka, :]) + _dot(s_ref[...], wbf[ka:, :])
        o_ref[...] = x_ref[...] + g_ref[...] * acc

    _on_streams(functools.partial(run, ap_ref, sp_ref, xp_ref, gp_ref, op_ref),
                functools.partial(run, as_ref, ss_ref, xs_ref, gs_ref, os_ref))


def mm_outproj(attn, ssm, w, layer, x, g, tm=1024, tn=512):
    b, t, ka = attn[0].shape
    ms = attn[1].shape[0]
    _, k, n = w.shape
    tm = min(tm, t)
    steps, pidx = _streams(b, t, tm)
    ap, as_ = _stream_specs(pidx, tm, ms, ka)
    sp, ss = _stream_specs(pidx, tm, ms, k - ka)
    xp, xs = _stream_specs(pidx, tm, ms, n, tn)
    gp = pl.BlockSpec((None, 1, tn), lambda j, s: (pidx(s)[0], 0, j))
    return pl.pallas_call(
        _outproj_kernel,
        grid=(n // tn, steps),
        in_specs=[ap, sp, as_, ss, pl.BlockSpec((None, k, tn), lambda j, s: (layer, 0, j)), xp, xs, gp, xs],
        out_specs=[xp, xs],
        out_shape=[jax.ShapeDtypeStruct((b, t, n), F32), jax.ShapeDtypeStruct((ms, n), F32)],
        scratch_shapes=[pltpu.VMEM((k, tn), BF16)],
        compiler_params=_cparams(2),
        name="mm_outproj",
    )(attn[0], ssm[0], attn[1], ssm[1], w, x[0], x[1], g[0], g[1])


def _gateup_kernel(layer, n_tiles, tail, hp_ref, hs_ref, wg_hbm, wu_hbm, wd_ref, op_ref, os_ref, wdbf_ref,
                   stage_g, stage_u, wgbf, wubf, sems):
    j = pl.program_id(0)
    tn = stage_g.shape[1]

    def copies(jj, width):
        col = jj * tn
        return [pltpu.make_async_copy(w.at[layer, :, pl.ds(col, width)], st.at[:, pl.ds(0, width)], sems.at[i])
                for i, (w, st) in enumerate(((wg_hbm, stage_g), (wu_hbm, stage_u)))]

    start, wait = _staged_blocks(n_tiles, tail, tn, copies)

    @pl.when(pl.program_id(1) == 0)
    def _():
        pl.when(j == 0)(functools.partial(start, j))
        wait(j)

        def cast_rows(i, carry):
            r = pl.multiple_of(i * CAST_ROWS, CAST_ROWS)
            wgbf[pl.ds(r, CAST_ROWS), :] = stage_g[pl.ds(r, CAST_ROWS), :].astype(BF16)
            wubf[pl.ds(r, CAST_ROWS), :] = stage_u[pl.ds(r, CAST_ROWS), :].astype(BF16)
            return carry

        lax.fori_loop(0, stage_g.shape[0] // CAST_ROWS, cast_rows, 0)
        pl.when(j + 1 < n_tiles)(functools.partial(start, j + 1))

    wdbf_ref[...] = wd_ref[...].astype(BF16)

    def run(h_ref, o_ref):
        for c in range(tn // MXU_WIDTH):
            cols = slice(c * MXU_WIDTH, (c + 1) * MXU_WIDTH)

            def one_width(cols=cols):
                h = h_ref[...]
                o_ref[:, cols] = (_silu(_dot(h, wgbf[:, cols])) * _dot(h, wubf[:, cols])).astype(o_ref.dtype)

            pl.when(j >= 0)(one_width)

    _on_streams(functools.partial(run, hp_ref, op_ref), functools.partial(run, hs_ref, os_ref))


def _ride_along_rows(total_rows, n_steps):
    align = 2 * SUBLANES
    return next(r for r in range(align, total_rows + 1, align) if total_rows % r == 0 and total_rows // r <= n_steps)


def mm_gateup(hp, hs, wg, wu, wd, layer, tm=1024, tn=512):
    b, t, k = hp.shape
    ms = hs.shape[0]
    n = wg.shape[2]
    _, kd, nd = wd.shape
    tm = min(tm, t)
    steps, pidx = _streams(b, t, tm)
    n_tiles = pl.cdiv(n, tn)
    tail = n - (n_tiles - 1) * tn
    assert n_tiles >= 2 and tail % LANES == 0 and tn % MXU_WIDTH == 0 and k % CAST_ROWS == 0
    rows = _ride_along_rows(kd, n_tiles * steps)
    wd_row = lambda j, s: jnp.minimum(j * steps + s, kd // rows - 1)
    hbm = pl.BlockSpec(memory_space=pltpu.HBM)
    return pl.pallas_call(
        functools.partial(_gateup_kernel, layer, n_tiles, tail),
        grid=(n_tiles, steps),
        in_specs=[*_stream_specs(pidx, tm, ms, k), hbm, hbm,
                  pl.BlockSpec((None, rows, nd), lambda j, s: (layer, wd_row(j, s), 0))],
        out_specs=[*_stream_specs(pidx, tm, ms, n, tn),
                   pl.BlockSpec((rows, nd), lambda j, s: (wd_row(j, s), 0))],
        out_shape=[jax.ShapeDtypeStruct((b, t, n), BF16), jax.ShapeDtypeStruct((ms, n), BF16),
                   jax.ShapeDtypeStruct((kd, nd), BF16)],
        scratch_shapes=[pltpu.VMEM((k, tn), F32), pltpu.VMEM((k, tn), F32),
                        pltpu.VMEM((k, tn), BF16), pltpu.VMEM((k, tn), BF16),
                        pltpu.SemaphoreType.DMA((2,))],
        compiler_params=_cparams(2),
        name="mm_gateup",
    )(hp, hs, wg, wu, wd)


def _down_kernel(h_ref, w_ref, x_ref, g_ref, o_ref):
    o_ref[...] = x_ref[...] + g_ref[...] * _dot(h_ref[...], w_ref[...])


def mm_down(hid, wbf, x, g, tm_cap=512, tn=512):
    b, t, k = hid.shape
    n = wbf.shape[1]
    tm = _row_tile(t, tm_cap)
    gspec = (pl.BlockSpec((None, 1, tn), lambda i, m, j: (i, 0, j)) if g.shape[1] == 1
             else pl.BlockSpec((None, tm, tn), lambda i, m, j: (i, m, j)))
    return pl.pallas_call(
        _down_kernel,
        grid=(b, t // tm, n // tn),
        in_specs=[pl.BlockSpec((None, tm, k), lambda i, m, j: (i, m, 0)),
                  pl.BlockSpec((k, tn), lambda i, m, j: (0, j)),
                  pl.BlockSpec((None, tm, tn), lambda i, m, j: (i, m, j)),
                  gspec],
        out_specs=pl.BlockSpec((None, tm, tn), lambda i, m, j: (i, m, j)),
        out_shape=jax.ShapeDtypeStruct((b, t, n), F32),
        compiler_params=_cparams(3),
        name="mm_down",
    )(hid, wbf, x, g)


def rope_tables(pos):
    half = ROT_DIM // 2
    inv = ROPE_THETA ** (-jnp.arange(0, ROT_DIM, 2, dtype=F32) / ROT_DIM)
    ang = pos.astype(F32)[:, None] * inv[None, :]
    cos, sin = jnp.cos(ang), jnp.sin(ang)
    t = pos.shape[0]
    rest = HEAD_DIM - ROT_DIM
    c = jnp.concatenate([cos, cos, jnp.ones((t, rest), F32)], axis=1)
    s_up = jnp.concatenate([-sin, jnp.zeros((t, half + rest), F32)], axis=1)
    s_dn = jnp.concatenate([jnp.zeros((t, half), F32), sin, jnp.zeros((t, rest), F32)], axis=1)
    reps = LANES // HEAD_DIM
    return tuple(jnp.tile(a, (1, reps)) for a in (c, s_up, s_dn))


def _rope(x, c, s_up, s_dn):
    w = x.shape[-1]
    reps = w // LANES
    half = ROT_DIM // 2
    tile = lambda a: jnp.concatenate([a] * reps, axis=1) if reps > 1 else a
    return (x * tile(c) + pltpu.roll(x, w - half, 1) * tile(s_up) + pltpu.roll(x, half, 1) * tile(s_dn))


def _low_head(shape):
    return lax.broadcasted_iota(jnp.int32, shape, len(shape) - 1) % LANES < HEAD_DIM


def _swa_prompt_kernel(sinks_ref, q_ref, kv_ref, c_ref, su_ref, sd_ref, o_ref, nk_ref, nv_ref, kprev, vprev):
    n = pl.program_id(1)
    last = pl.num_programs(1) - 1
    blk = q_ref.shape[0]

    @pl.when(n == 0)
    def _():
        kprev[...] = jnp.zeros_like(kprev)
        vprev[...] = jnp.zeros_like(vprev)

    c, su, sd = c_ref[...], su_ref[...], sd_ref[...]
    q = _rope(q_ref[...], c, su, sd) * (HEAD_DIM ** -0.5)
    low = _low_head(q.shape)
    q_heads = (jnp.where(low, q, 0.0).astype(BF16), jnp.where(low, 0.0, q).astype(BF16))
    kv = kv_ref[...]
    k = _rope(kv[:, :KV_W], c, su, sd)
    v = kv[:, KV_W:]

    @pl.when(n == last)
    def _():
        nk_ref[...] = k
        nv_ref[...] = v

    kk = jnp.concatenate([kprev[(n + 1) % 2], k], axis=0)
    vv = jnp.concatenate([vprev[(n + 1) % 2], v], axis=0)
    low2 = _low_head((2 * blk, LANES))
    low1 = _low_head((blk, LANES))
    row = lax.broadcasted_iota(jnp.int32, (blk, blk), 0)
    col = lax.broadcasted_iota(jnp.int32, (blk, blk), 1)
    from_prev = col > row
    no_prev = jnp.where(n > 0, 0.0, -jnp.inf)

    def both_halves(x, g):
        tile = x[:, (g // 2) * LANES:(g // 2 + 1) * LANES]
        other = pltpu.roll(tile, HEAD_DIM, 1)
        return (jnp.where(low2, other, tile) if g % 2 else jnp.where(low2, tile, other)).astype(BF16)

    for g in range(N_KV):
        kd, vd = both_halves(kk, g), both_halves(vv, g)
        for jp in range(Q_PER_KV // 2):
            lanes = slice((g * Q_PER_KV // 2 + jp) * LANES, (g * Q_PER_KV // 2 + jp + 1) * LANES)
            outs = []
            for half in range(2):
                h = g * Q_PER_KV + 2 * jp + half
                s = _dot_nt(q_heads[half][:, lanes], kd)
                s = jnp.where(from_prev, s[:, :blk] + no_prev, s[:, blk:])
                sink = sinks_ref[h]
                m = jnp.maximum(jnp.max(s, axis=-1, keepdims=True), sink)
                e = jnp.exp(s - m)
                den = jnp.sum(e, axis=-1, keepdims=True) + jnp.exp(sink - m)
                pe = jnp.concatenate([jnp.where(from_prev, e, 0.0), jnp.where(from_prev, 0.0, e)], axis=1)
                outs.append(_dot(pe.astype(BF16), vd) * (1.0 / den))
            o_ref[:, lanes] = jnp.where(low1, outs[0], outs[1]).astype(o_ref.dtype)
    kprev[n % 2] = k
    vprev[n % 2] = v


def swa_prompt(proj, sinks, tables):
    b, t, _ = proj.shape
    blk = WINDOW
    nb = t // blk
    kv_col = OFF_K // (2 * KV_W)
    tab_spec = pl.BlockSpec((blk, LANES), lambda i, n: (n, 0))
    keep_spec = pl.BlockSpec((None, blk, KV_W), lambda i, n: (i, 0, 0))
    return pl.pallas_call(
        _swa_prompt_kernel,
        grid=(b, nb),
        in_specs=[pl.BlockSpec(memory_space=pltpu.SMEM),
                  pl.BlockSpec((None, blk, Q_W), lambda i, n: (i, n, 0)),
                  pl.BlockSpec((None, blk, 2 * KV_W), lambda i, n: (i, n, kv_col)),
                  tab_spec, tab_spec, tab_spec],
        out_specs=[pl.BlockSpec((None, blk, Q_W), lambda i, n: (i, n, 0)), keep_spec, keep_spec],
        out_shape=[jax.ShapeDtypeStruct((b, t, Q_W), BF16),
                   jax.ShapeDtypeStruct((b, blk, KV_W), F32),
                   jax.ShapeDtypeStruct((b, blk, KV_W), F32)],
        scratch_shapes=[pltpu.VMEM((2, blk, KV_W), F32), pltpu.VMEM((2, blk, KV_W), F32)],
        compiler_params=_cparams(2),
        name="swa_prompt",
    )(sinks, proj, proj, *tables)


def _swa_decode_kernel(t_new, sinks_ref, q_ref, kv_ref, kc_ref, vc_ref, c_ref, su_ref, sd_ref,
                       o_ref, nk_ref, nv_ref, kk, vv):
    rows = q_ref.shape[0]
    wc = kc_ref.shape[0]
    c, su, sd = c_ref[...], su_ref[...], sd_ref[...]
    q = (_rope(q_ref[...], c, su, sd) * (HEAD_DIM ** -0.5)).astype(BF16)
    kv = kv_ref[...]
    k = _rope(kv[:, :KV_W], c, su, sd)
    v = kv[:, KV_W:]
    kc, vc = kc_ref[...], vc_ref[...]

    nk_ref[0:wc - t_new, :] = kc[t_new:, :]
    nv_ref[0:wc - t_new, :] = vc[t_new:, :]
    nk_ref[wc - t_new:wc, :] = k[0:t_new, :]
    nv_ref[wc - t_new:wc, :] = v[0:t_new, :]

    kk[...] = jnp.zeros_like(kk)
    vv[...] = jnp.zeros_like(vv)
    kk[0:wc, :] = kc.astype(BF16)
    vv[0:wc, :] = vc.astype(BF16)
    kk[wc:wc + rows, :] = k.astype(BF16)
    vv[wc:wc + rows, :] = v.astype(BF16)

    nkeys = kk.shape[0]
    nrow = Q_PER_KV * rows
    r = lax.broadcasted_iota(jnp.int32, (nrow, nkeys), 0)
    sj = lax.broadcasted_iota(jnp.int32, (nrow, nkeys), 1)
    tq = r % rows
    mask = jnp.logical_and(jnp.logical_and(sj <= tq + wc, sj > tq + wc - WINDOW), sj < wc + t_new)
    rcol = lax.broadcasted_iota(jnp.int32, (nrow, 1), 0) // rows
    for g in range(N_KV):
        kg = kk[:, g * HEAD_DIM:(g + 1) * HEAD_DIM]
        vg = vv[:, g * HEAD_DIM:(g + 1) * HEAD_DIM]
        qs = jnp.concatenate([q[:, (g * Q_PER_KV + hh) * HEAD_DIM:(g * Q_PER_KV + hh + 1) * HEAD_DIM]
                              for hh in range(Q_PER_KV)], axis=0)
        sink = jnp.zeros((nrow, 1), F32)
        for hh in range(Q_PER_KV):
            sink = jnp.where(rcol == hh, sinks_ref[g * Q_PER_KV + hh], sink)
        s = jnp.where(mask, _dot_nt(qs, kg), -jnp.inf)
        m = jnp.maximum(jnp.max(s, axis=-1, keepdims=True), sink)
        e = jnp.exp(s - m)
        p = e / (jnp.sum(e, axis=-1, keepdims=True) + jnp.exp(sink - m))
        og = _dot(p.astype(BF16), vg)
        for hh in range(Q_PER_KV):
            h = g * Q_PER_KV + hh
            o_ref[:, h * HEAD_DIM:(h + 1) * HEAD_DIM] = og[hh * rows:(hh + 1) * rows, :].astype(o_ref.dtype)


def swa_decode(proj, sinks, cache_k, cache_v, layer, tables, t_new):
    b, rows, _ = proj.shape
    wc = cache_k.shape[2]
    kv_col = OFF_K // (2 * KV_W)
    tab_spec = pl.BlockSpec((rows, LANES), lambda i: (0, 0))
    cache_in = pl.BlockSpec((None, None, wc, KV_W), lambda i: (layer, i, 0, 0))
    cache_out = pl.BlockSpec((None, wc, KV_W), lambda i: (i, 0, 0))
    nkeys = 2 * WINDOW
    return pl.pallas_call(
        functools.partial(_swa_decode_kernel, t_new),
        grid=(b,),
        in_specs=[pl.BlockSpec(memory_space=pltpu.SMEM),
                  pl.BlockSpec((None, rows, Q_W), lambda i: (i, 0, 0)),
                  pl.BlockSpec((None, rows, 2 * KV_W), lambda i: (i, 0, kv_col)),
                  cache_in, cache_in, tab_spec, tab_spec, tab_spec],
        out_specs=[pl.BlockSpec((None, rows, Q_W), lambda i: (i, 0, 0)), cache_out, cache_out],
        out_shape=[jax.ShapeDtypeStruct((b, rows, Q_W), BF16),
                   jax.ShapeDtypeStruct((b, wc, KV_W), F32),
                   jax.ShapeDtypeStruct((b, wc, KV_W), F32)],
        scratch_shapes=[pltpu.VMEM((nkeys, KV_W), BF16), pltpu.VMEM((nkeys, KV_W), BF16)],
        compiler_params=_cparams(1),
        name="swa_decode",
    )(sinks, proj, proj, cache_k, cache_v, *tables)


def spread_matrices():
    src = jnp.arange(LANES)
    head = src % N_SSM_HEADS
    live = src < SPLIT_PARTS * N_SSM_HEADS
    to_x = (head[:, None] == (jnp.arange(D_SSM) // SSM_HEAD_DIM)[None, :]) & live[:, None]
    to_seg = (head[:, None] == (jnp.arange(N_SSM_HEADS * CHUNK) // CHUNK)[None, :]) & live[:, None]
    return to_x.astype(BF16), to_seg.astype(BF16)


def _three_copies(x):
    return x + pltpu.roll(x, N_SSM_HEADS, 1) + pltpu.roll(x, 2 * N_SSM_HEADS, 1)


def _split_parts(x3, part):
    hi = x3.astype(BF16).astype(F32)
    r1 = x3 - hi
    mid = r1.astype(BF16).astype(F32)
    lo = r1 - mid
    return jnp.where(part == 0, hi, jnp.where(part == 1, mid, jnp.where(part == 2, lo, 0.0))).astype(BF16)


def _ssd_kernel(valid, proj_ref, cinit_ref, h0_ref, cw_ref, cb_ref, dtb_ref, alog_ref, dsk_ref, nw_ref,
                tox_ref, toseg_ref, o_ref, tail_ref, hn_ref, xp, ybuf):
    cidx = pl.program_id(1)
    qin = proj_ref.shape[0]
    q = ybuf.shape[0]
    pad = q - qin
    hist = SUBLANES

    @pl.when(cidx == 0)
    def _():
        xp[0:hist, :] = cinit_ref[...]
        hn_ref[...] = h0_ref[...]

    blk = proj_ref[...]
    xp[hist:hist + qin, :] = blk[:, OFF_XBC:OFF_DT]
    if pad:
        xp[hist + qin:, :] = jnp.zeros((pad, CONV_DIM), F32)
    tail_ref[...] = xp[valid:valid + hist, :]

    cw = cw_ref[...]
    conv = cb_ref[...] + cw[CONV_W - 1:CONV_W, :] * xp[hist:hist + q, :]
    for j in range(CONV_W - 1):
        shift = CONV_W - 1 - j
        conv = conv + cw[j:j + 1, :] * xp[hist - shift:hist - shift + q, :]
    xc = _silu(conv)
    xp[0:hist, :] = xp[q:q + hist, :]

    xs = xc[:, :D_SSM]
    gw = D_STATE
    bmat = xc[:, D_SSM:D_SSM + N_SSM_GROUPS * gw].astype(BF16)
    cmat = xc[:, D_SSM + N_SSM_GROUPS * gw:].astype(BF16)

    lane = lax.broadcasted_iota(jnp.int32, (q, LANES), 1)
    rowi = lax.broadcasted_iota(jnp.int32, (q, LANES), 0)
    part = lane // N_SSM_HEADS
    dt_raw = blk[:, OFF_DT:OFF_DT + LANES]
    if pad:
        dt_raw = jnp.concatenate([dt_raw, jnp.zeros((pad, LANES), F32)], axis=0)
    dt_raw = _three_copies(jnp.where(lane < N_SSM_HEADS, dt_raw, 0.0))
    xdtb = dt_raw + dtb_ref[...]
    dt = jnp.maximum(xdtb, 0.0) + jnp.log1p(jnp.exp(-jnp.abs(xdtb)))
    dt = jnp.where(jnp.logical_and(rowi < valid, part < SPLIT_PARTS), dt, 0.0)
    a = dt * (-jnp.exp(alog_ref[...]))

    li = lax.broadcasted_iota(jnp.int32, (q, q), 0)
    si = lax.broadcasted_iota(jnp.int32, (q, q), 1)
    causal = li >= si
    tri = causal.astype(BF16)
    triu = (li <= si).astype(BF16)
    pa = _split_parts(a, part)
    c3 = _dot(tri, pa)
    acum = c3
    for k in range(1, LANES // N_SSM_HEADS):
        acum = acum + pltpu.roll(c3, k * N_SSM_HEADS, 1)
    ct = _dot_tn(pa, triu)
    acum_t = ct[0:N_SSM_HEADS] + ct[N_SSM_HEADS:2 * N_SSM_HEADS] + ct[2 * N_SSM_HEADS:3 * N_SSM_HEADS]
    alast = acum[q - 1:q, :]
    e_acum = jnp.exp(acum)
    e_chunk = e_acum[q - 1:q, :]

    tox = tox_ref[...]
    dtx = _dot(_split_parts(dt, part), tox)
    e_end_x = _dot(_split_parts(jnp.exp(alast - acum), part), tox)
    e_acum_x = _dot(_split_parts(e_acum, part), tox)
    pacum = _split_parts(acum, part)

    xdt = xs * dtx
    low = _low_head(xdt.shape)
    xdt_heads = (jnp.where(low, xdt, 0.0).astype(BF16), jnp.where(low, 0.0, xdt).astype(BF16))
    xe = (xdt * e_end_x).astype(BF16)
    ybuf[...] = dsk_ref[...] * xs

    gsz = SSM_R * SSM_HEAD_DIM
    for g in range(N_SSM_GROUPS):
        bg = bmat[:, g * gw:(g + 1) * gw]
        cg = cmat[:, g * gw:(g + 1) * gw]
        glanes = slice(g * gsz, (g + 1) * gsz)
        grows = slice(g * gsz, (g + 1) * gsz)
        cb = _dot_nt(cg, bg)
        hold = hn_ref[grows, :]
        ybuf[:, glanes] += _dot_nt(cg, hold.astype(BF16)) * e_acum_x[:, glanes]
        seg = _dot(pacum, toseg_ref[:, g * SSM_R * q:(g + 1) * SSM_R * q])
        for jp in range(SSM_R // 2):
            ms = []
            for half in range(2):
                r = 2 * jp + half
                h = g * SSM_R + r
                d = seg[:, r * q:(r + 1) * q] - acum_t[h:h + 1, :]
                ms.append((cb * jnp.exp(jnp.where(causal, d, -jnp.inf))).astype(BF16))
            plane = slice(g * gsz + jp * LANES, g * gsz + (jp + 1) * LANES)
            rhs = jnp.concatenate([xdt_heads[0][:, plane], xdt_heads[1][:, plane]], axis=0)
            ybuf[:, plane] += _dot(jnp.concatenate(ms, axis=1), rhs)
        st = _dot_tn(xe[:, glanes], bg)
        for r in range(SSM_R):
            h = g * SSM_R + r
            rows = slice(h * SSM_HEAD_DIM, (h + 1) * SSM_HEAD_DIM)
            hn_ref[rows, :] = (hold[r * SSM_HEAD_DIM:(r + 1) * SSM_HEAD_DIM, :] * e_chunk[:, h:h + 1]
                               + st[r * SSM_HEAD_DIM:(r + 1) * SSM_HEAD_DIM, :])

    z = blk[:, OFF_Z:OFF_XBC]
    gated = ybuf[0:qin, :] * _silu(z)
    nsz = D_SSM // N_SSM_GROUPS
    nw = nw_ref[...]
    for g in range(N_SSM_GROUPS):
        gg = gated[:, g * nsz:(g + 1) * nsz]
        gg = gg * lax.rsqrt(jnp.mean(gg * gg, axis=-1, keepdims=True) + EPS)
        o_ref[:, g * nsz:(g + 1) * nsz] = (gg * nw[:, g * nsz:(g + 1) * nsz]).astype(o_ref.dtype)


def ssd_mixer(proj, conv_init, h0, layer, conv_w, conv_b, dt_bias, a_log, d_skip, norm_w, spread, qin, valid):
    b, t, _ = proj.shape
    nc = t // qin
    copies = lambda a: jnp.concatenate([a] * SPLIT_PARTS + [jnp.zeros_like(a)]).reshape(1, LANES)
    const2 = lambda shape: pl.BlockSpec(shape, lambda i, c: (0, 0))
    state_in = pl.BlockSpec((None, None, STATE_ROWS, D_STATE), lambda i, c: (layer, i, 0, 0))
    state_out = pl.BlockSpec((None, STATE_ROWS, D_STATE), lambda i, c: (i, 0, 0))
    hist_spec = pl.BlockSpec((None, SUBLANES, CONV_DIM), lambda i, c: (i, 0, 0))
    to_x, to_seg = spread
    return pl.pallas_call(
        functools.partial(_ssd_kernel, valid),
        grid=(b, nc),
        in_specs=[pl.BlockSpec((None, qin, P_PAD), lambda i, c: (i, c, 0)),
                  hist_spec, state_in,
                  const2((CONV_W, CONV_DIM)), const2((1, CONV_DIM)), const2((1, LANES)),
                  const2((1, LANES)), const2((1, D_SSM)), const2((1, D_SSM)),
                  const2(to_x.shape), const2(to_seg.shape)],
        out_specs=[pl.BlockSpec((None, qin, D_SSM), lambda i, c: (i, c, 0)), hist_spec, state_out],
        out_shape=[jax.ShapeDtypeStruct((b, t, D_SSM), BF16),
                   jax.ShapeDtypeStruct((b, SUBLANES, CONV_DIM), F32),
                   jax.ShapeDtypeStruct((b, STATE_ROWS, D_STATE), F32)],
        scratch_shapes=[pltpu.VMEM((SUBLANES + CHUNK, CONV_DIM), F32), pltpu.VMEM((CHUNK, D_SSM), F32)],
        compiler_params=_cparams(2),
        name="ssd_mixer",
    )(proj, conv_init, h0, conv_w, conv_b.reshape(1, -1), copies(dt_bias), copies(a_log),
      jnp.repeat(d_skip, SSM_HEAD_DIM).reshape(1, -1), norm_w.reshape(1, -1), to_x, to_seg)


def _layer(xp, xs, mod_p, mod_s, weights, layer, attn_p, attn_s, conv_init_s, h0_s, spread, zeros_p, t_new):
    (norm1_w, w_in_t, conv_w, conv_b, dt_bias, a_log, d_skip, ssm_norm_w, w_out, norm2_w, w_gate, w_up,
     w_down) = weights
    ms = xs.shape[0]
    bs = ms // SAMPLE_ROWS
    as3 = lambda a: a[None]
    hp = norm_mod(xp, norm1_w[layer], mod_p[1], mod_p[0])
    hs = norm_mod(as3(xs), norm1_w[layer], as3(mod_s[1]), as3(mod_s[0]))[0]
    proj_p, proj_s = mm_inproj(hp, hs, w_in_t, layer)
    proj_s = proj_s.reshape(bs, SAMPLE_ROWS, P_PAD)

    ssd = functools.partial(ssd_mixer, conv_w=conv_w[layer], conv_b=conv_b[layer], dt_bias=dt_bias[layer],
                            a_log=a_log[layer], d_skip=d_skip[layer], norm_w=ssm_norm_w[layer], spread=spread)
    oa_p, k_p, v_p = attn_p(proj_p)
    os_p, tail_p, hn_p = ssd(proj_p, zeros_p[0], zeros_p[1], 0, qin=CHUNK, valid=CHUNK)
    oa_s, k_s, v_s = attn_s(proj_s)
    os_s, tail_s, hn_s = ssd(proj_s, conv_init_s, h0_s, layer, qin=SAMPLE_ROWS, valid=t_new)

    xp, xs = mm_outproj((oa_p, oa_s.reshape(ms, Q_W)), (os_p, os_s.reshape(ms, D_SSM)), w_out, layer,
                        (xp, xs), (mod_p[2], mod_s[2]))
    h2p = norm_mod(xp, norm2_w[layer], mod_p[4], mod_p[3])
    h2s = norm_mod(as3(xs), norm2_w[layer], as3(mod_s[4]), as3(mod_s[3]))[0]
    hid_p, hid_s, wdown_bf = mm_gateup(h2p, h2s, w_gate, w_up, w_down, layer)
    xp = mm_down(hid_p, wdown_bf, xp, mod_p[5])
    xs = mm_down(as3(hid_s), wdown_bf, as3(xs), as3(mod_s[5]))[0]

    keep = SUBLANES - (CONV_W - 1)
    state = lambda h: h.reshape(-1, N_SSM_HEADS, SSM_HEAD_DIM, D_STATE)
    heads = lambda a: a.reshape(a.shape[0], a.shape[1], N_KV, HEAD_DIM)
    return xp, xs, (heads(k_p), heads(v_p), tail_p[:, keep:], state(hn_p),
                    heads(k_s), heads(v_s), tail_s[:, keep:], state(hn_s))


def kernel(x_prompt, x_sample, cache_k, cache_v, state_conv, state_ssm, c_prompt, c_sample, w_ada, b_ada,
           norm1_w, w_in, conv_w, conv_b, dt_bias, a_log, d_skip, ssm_norm_w, sinks, w_out, norm2_w,
           w_gate, w_up, w_down, final_norm_w):
    depth = w_ada.shape[0]
    bp, tp, d = x_prompt.shape
    bs, ts, _ = x_sample.shape
    wc = cache_k.shape[2]
    assert ts <= SAMPLE_ROWS and tp % CHUNK == 0 and ts % CHUNK != 0

    c_all = jnp.concatenate([c_prompt, c_sample, jnp.zeros((ADA_ROWS - bp - bs, d), F32)], axis=0)
    mod_all = ada_mod(c_all, w_ada, b_ada)

    tab_p = rope_tables(jnp.arange(tp, dtype=jnp.int32))
    tab_s = rope_tables(PAST_LEN + jnp.arange(SAMPLE_ROWS, dtype=jnp.int32))
    spread = spread_matrices()
    weights = (norm1_w, jnp.swapaxes(w_in, 1, 2), conv_w, conv_b, dt_bias, a_log, d_skip, ssm_norm_w, w_out,
               norm2_w, w_gate, w_up, w_down)

    xp = x_prompt
    xs = jnp.pad(x_sample, ((0, 0), (0, SAMPLE_ROWS - ts), (0, 0))).reshape(bs * SAMPLE_ROWS, d)
    zeros_p = (jnp.zeros((bp, SUBLANES, CONV_DIM), F32), jnp.zeros((1, bp, STATE_ROWS, D_STATE), F32))
    h0_s = state_ssm.reshape(depth, bs, STATE_ROWS, D_STATE)
    ck = cache_k.reshape(depth, bs, wc, KV_W)
    cv = cache_v.reshape(depth, bs, wc, KV_W)

    outs = []
    for l in range(depth):
        mod_p = [mod_all[l, :bp, i * d:(i + 1) * d][:, None, :] for i in range(6)]
        mod_s = [jnp.repeat(mod_all[l, bp:bp + bs, i * d:(i + 1) * d], SAMPLE_ROWS, axis=0) for i in range(6)]
        attn_p = functools.partial(swa_prompt, sinks=sinks[l], tables=tab_p)
        attn_s = functools.partial(swa_decode, sinks=sinks[l], cache_k=ck, cache_v=cv, layer=l, tables=tab_s,
                                   t_new=ts)
        conv_init_s = jnp.pad(state_conv[l], ((0, 0), (SUBLANES - (CONV_W - 1), 0), (0, 0)))
        xp, xs, layer_outs = _layer(xp, xs, mod_p, mod_s, weights, l, attn_p, attn_s, conv_init_s, h0_s, spread,
                                    zeros_p, ts)
        outs.append(layer_outs)

    y_prompt = final_norm(xp, final_norm_w)
    y_sample = final_norm(xs[None], final_norm_w).reshape(bs, SAMPLE_ROWS, d)[:, :ts]
    return (y_prompt, y_sample) + tuple(jnp.stack(o) for o in zip(*outs))
```

```python
import functools

import jax
import jax.numpy as jnp
from jax import lax
from jax.experimental import pallas as pl
from jax.experimental.pallas import tpu as pltpu

F32 = jnp.float32
BF16 = jnp.bfloat16

D_MODEL = 4096
HEAD_DIM = 64
N_HEADS = 32
N_KV = 4
Q_PER_KV = N_HEADS // N_KV
WINDOW = 128
ROT_DIM = 16
ROPE_THETA = 500000.0
PAST_LEN = 16384
D_SSM = 2048
N_SSM_HEADS = 32
SSM_HEAD_DIM = 64
N_SSM_GROUPS = 4
SSM_R = N_SSM_HEADS // N_SSM_GROUPS
D_STATE = 128
CONV_W = 4
CONV_DIM = D_SSM + 2 * N_SSM_GROUPS * D_STATE
CHUNK = 128
D_FF = 11008
EPS = 1e-6
Q_W = N_HEADS * HEAD_DIM
KV_W = N_KV * HEAD_DIM
OFF_K = Q_W
OFF_V = Q_W + KV_W
OFF_Z = Q_W + 2 * KV_W
OFF_XBC = OFF_Z + D_SSM
OFF_DT = OFF_XBC + CONV_DIM
P_IN = OFF_DT + N_SSM_HEADS

LANES = 128
SUBLANES = 8
MXU_WIDTH = 256
CAST_ROWS = 512
VMEM_LIMIT_BYTES = 56 * 1024 * 1024

P_PAD = -(-P_IN // LANES) * LANES
DECODE_SEQS_PER_STEP = 4
SAMPLE_ROWS = 8
ADA_ROWS = 40
SPLIT_PARTS = 3
STATE_ROWS = N_SSM_HEADS * SSM_HEAD_DIM


def _cparams(n_axes):
    return pltpu.CompilerParams(dimension_semantics=("arbitrary",) * n_axes, vmem_limit_bytes=VMEM_LIMIT_BYTES)


def _silu(x):
    return x * jax.nn.sigmoid(x)


def _dot(a, b):
    return jnp.dot(a, b, preferred_element_type=F32)


def _dot_nt(a, b):
    return lax.dot_general(a, b, (((1,), (1,)), ((), ())), preferred_element_type=F32)


def _dot_tn(a, b):
    return lax.dot_general(a, b, (((0,), (0,)), ((), ())), preferred_element_type=F32)


def _ada_kernel(c_ref, w_ref, b_ref, o_ref):
    x = _silu(c_ref[...]).astype(BF16)
    w = w_ref[...].astype(BF16)
    o_ref[...] = _dot(x, w) + b_ref[...]


def ada_mod(c_all, w_ada, b_ada, tn=1024):
    depth, d, n = w_ada.shape
    rows = c_all.shape[0]
    return pl.pallas_call(
        _ada_kernel,
        grid=(depth, n // tn),
        in_specs=[
            pl.BlockSpec((rows, d), lambda l, j: (0, 0)),
            pl.BlockSpec((None, d, tn), lambda l, j: (l, 0, j)),
            pl.BlockSpec((None, 1, tn), lambda l, j: (l, 0, j)),
        ],
        out_specs=pl.BlockSpec((None, rows, tn), lambda l, j: (l, 0, j)),
        out_shape=jax.ShapeDtypeStruct((depth, rows, n), F32),
        compiler_params=_cparams(2),
        name="ada_mod",
    )(c_all, w_ada, b_ada.reshape(depth, 1, n))


def _norm_mod_kernel(x_ref, nw_ref, sc_ref, sh_ref, o_ref):
    x = x_ref[...]
    y = x * lax.rsqrt(jnp.mean(x * x, axis=-1, keepdims=True) + EPS) * nw_ref[...]
    o_ref[...] = (y * (1.0 + sc_ref[...]) + sh_ref[...]).astype(o_ref.dtype)


def _norm_kernel(x_ref, nw_ref, o_ref):
    x = x_ref[...]
    o_ref[...] = x * lax.rsqrt(jnp.mean(x * x, axis=-1, keepdims=True) + EPS) * nw_ref[...]


def _row_tile(t, cap):
    tt = min(t, cap)
    assert t % tt == 0
    return tt


def norm_mod(x, nw, sc, sh, tt_cap=512):
    b, t, d = x.shape
    tt = _row_tile(t, tt_cap)
    mt = sc.shape[1]
    mod_spec = (pl.BlockSpec((None, 1, d), lambda i, j: (i, 0, 0)) if mt == 1
                else pl.BlockSpec((None, tt, d), lambda i, j: (i, j, 0)))
    return pl.pallas_call(
        _norm_mod_kernel,
        grid=(b, t // tt),
        in_specs=[pl.BlockSpec((None, tt, d), lambda i, j: (i, j, 0)),
                  pl.BlockSpec((1, d), lambda i, j: (0, 0)), mod_spec, mod_spec],
        out_specs=pl.BlockSpec((None, tt, d), lambda i, j: (i, j, 0)),
        out_shape=jax.ShapeDtypeStruct((b, t, d), BF16),
        compiler_params=_cparams(2),
        name="norm_mod",
    )(x, nw.reshape(1, d), sc, sh)


def final_norm(x, nw, tt_cap=256):
    b, t, d = x.shape
    tt = _row_tile(t, tt_cap)
    return pl.pallas_call(
        _norm_kernel,
        grid=(b, t // tt),
        in_specs=[pl.BlockSpec((None, tt, d), lambda i, j: (i, j, 0)),
                  pl.BlockSpec((1, d), lambda i, j: (0, 0))],
        out_specs=pl.BlockSpec((None, tt, d), lambda i, j: (i, j, 0)),
        out_shape=jax.ShapeDtypeStruct((b, t, d), F32),
        compiler_params=_cparams(2),
        name="final_norm",
    )(x, nw.reshape(1, d))


def _streams(b, t, tm):
    assert t % tm == 0
    per_b = t // tm
    steps = b * per_b

    def prompt_idx(s):
        return s // per_b, s % per_b

    return steps, prompt_idx


def _stream_specs(pidx, tm, ms, width, col=None):
    if col is None:
        return (pl.BlockSpec((None, tm, width), lambda j, s: (*pidx(s), 0)),
                pl.BlockSpec((ms, width), lambda j, s: (0, 0)))
    return (pl.BlockSpec((None, tm, col), lambda j, s: (*pidx(s), j)),
            pl.BlockSpec((ms, col), lambda j, s: (0, j)))


def _on_streams(prompt_fn, sample_fn):
    prompt_fn()
    pl.when(pl.program_id(1) == pl.num_programs(1) - 1)(sample_fn)


def _staged_blocks(n_tiles, tail, full, copies):
    def for_block(jj, action):
        def run(block, extent):
            for c in copies(block, extent):
                action(c)

        if tail == full:
            run(jj, full)
        else:
            pl.when(jj < n_tiles - 1)(functools.partial(run, jj, full))
            pl.when(jj == n_tiles - 1)(functools.partial(run, n_tiles - 1, tail))

    return (functools.partial(for_block, action=lambda c: c.start()),
            functools.partial(for_block, action=lambda c: c.wait()))


def _inproj_kernel(layer, n_tiles, tail, hp_ref, hs_ref, wt_hbm, op_ref, os_ref, stage, wbf, sems):
    j = pl.program_id(0)
    tn = stage.shape[0]

    def copies(jj, rows):
        return [pltpu.make_async_copy(wt_hbm.at[layer, pl.ds(jj * tn, rows), :], stage.at[pl.ds(0, rows), :],
                                      sems.at[0])]

    start, wait = _staged_blocks(n_tiles, tail, tn, copies)

    @pl.when(pl.program_id(1) == 0)
    def _():
        pl.when(j == 0)(functools.partial(start, j))
        wait(j)
        for c in range(tn // MXU_WIDTH):
            w = stage[c * MXU_WIDTH:(c + 1) * MXU_WIDTH, :]
            row = j * tn + c * MXU_WIDTH + lax.broadcasted_iota(jnp.int32, w.shape, 0)
            wbf[:, c * MXU_WIDTH:(c + 1) * MXU_WIDTH] = jnp.where(row < P_IN, w, 0.0).T.astype(BF16)
        pl.when(j + 1 < n_tiles)(functools.partial(start, j + 1))

    def run(h_ref, o_ref):
        for c in range(tn // MXU_WIDTH):
            cols = slice(c * MXU_WIDTH, (c + 1) * MXU_WIDTH)

            def one_width(cols=cols):
                o_ref[:, cols] = _dot(h_ref[...], wbf[:, cols])

            pl.when(j >= 0)(one_width)

    _on_streams(functools.partial(run, hp_ref, op_ref), functools.partial(run, hs_ref, os_ref))


def mm_inproj(hp, hs, wt, layer, tm=1024, tn=1024):
    b, t, k = hp.shape
    ms = hs.shape[0]
    tm = min(tm, t)
    steps, pidx = _streams(b, t, tm)
    n_tiles = pl.cdiv(P_PAD, tn)
    tail = P_IN - (n_tiles - 1) * tn
    assert n_tiles >= 2 and tail > 0 and tail % SUBLANES == 0 and tn % MXU_WIDTH == 0
    return pl.pallas_call(
        functools.partial(_inproj_kernel, layer, n_tiles, tail),
        grid=(n_tiles, steps),
        in_specs=[*_stream_specs(pidx, tm, ms, k), pl.BlockSpec(memory_space=pltpu.HBM)],
        out_specs=list(_stream_specs(pidx, tm, ms, P_PAD, tn)),
        out_shape=[jax.ShapeDtypeStruct((b, t, P_PAD), F32), jax.ShapeDtypeStruct((ms, P_PAD), F32)],
        scratch_shapes=[pltpu.VMEM((tn, k), F32), pltpu.VMEM((k, tn), BF16), pltpu.SemaphoreType.DMA((1,))],
        compiler_params=_cparams(2),
        name="mm_inproj",
    )(hp, hs, wt)


def _outproj_kernel(ap_ref, sp_ref, as_ref, ss_ref, w_ref, xp_ref, xs_ref, gp_ref, gs_ref, op_ref, os_ref, wbf):
    @pl.when(pl.program_id(1) == 0)
    def _():
        wbf[...] = w_ref[...].astype(BF16)

    def run(a_ref, s_ref, x_ref, g_ref, o_ref):
        ka = a_ref.shape[-1]
        acc = _dot(a_ref[...], wbf[:ka, :]) + _dot(s_ref[...], wbf[ka:, :])
        o_ref[...] = x_ref[...] + g_ref[...] * acc

    _on_streams(functools.partial(run, ap_ref, sp_ref, xp_ref, gp_ref, op_ref),
                functools.partial(run, as_ref, ss_ref, xs_ref, gs_ref, os_ref))


def mm_outproj(attn, ssm, w, layer, x, g, tm=1024, tn=512):
    b, t, ka = attn[0].shape
    ms = attn[1].shape[0]
    _, k, n = w.shape
    tm = min(tm, t)
    steps, pidx = _streams(b, t, tm)
    ap, as_ = _stream_specs(pidx, tm, ms, ka)
    sp, ss = _stream_specs(pidx, tm, ms, k - ka)
    xp, xs = _stream_specs(pidx, tm, ms, n, tn)
    gp = pl.BlockSpec((None, 1, tn), lambda j, s: (pidx(s)[0], 0, j))
    return pl.pallas_call(
        _outproj_kernel,
        grid=(n // tn, steps),
        in_specs=[ap, sp, as_, ss, pl.BlockSpec((None, k, tn), lambda j, s: (layer, 0, j)), xp, xs, gp, xs],
        out_specs=[xp, xs],
        out_shape=[jax.ShapeDtypeStruct((b, t, n), F32), jax.ShapeDtypeStruct((ms, n), F32)],
        scratch_shapes=[pltpu.VMEM((k, tn), BF16)],
        compiler_params=_cparams(2),
        name="mm_outproj",
    )(attn[0], ssm[0], attn[1], ssm[1], w, x[0], x[1], g[0], g[1])


def _gateup_kernel(layer, n_tiles, tail, hp_ref, hs_ref, wg_hbm, wu_hbm, wd_ref, op_ref, os_ref, wdbf_ref,
                   stage_g, stage_u, wgbf, wubf, sems):
    j = pl.program_id(0)
    tn = stage_g.shape[1]

    def copies(jj, width):
        col = jj * tn
        return [pltpu.make_async_copy(w.at[layer, :, pl.ds(col, width)], st.at[:, pl.ds(0, width)], sems.at[i])
                for i, (w, st) in enumerate(((wg_hbm, stage_g), (wu_hbm, stage_u)))]

    start, wait = _staged_blocks(n_tiles, tail, tn, copies)

    @pl.when(pl.program_id(1) == 0)
    def _():
        pl.when(j == 0)(functools.partial(start, j))
        wait(j)

        def cast_rows(i, carry):
            r = pl.multiple_of(i * CAST_ROWS, CAST_ROWS)
            wgbf[pl.ds(r, CAST_ROWS), :] = stage_g[pl.ds(r, CAST_ROWS), :].astype(BF16)
            wubf[pl.ds(r, CAST_ROWS), :] = stage_u[pl.ds(r, CAST_ROWS), :].astype(BF16)
            return carry

        lax.fori_loop(0, stage_g.shape[0] // CAST_ROWS, cast_rows, 0)
        pl.when(j + 1 < n_tiles)(functools.partial(start, j + 1))

    wdbf_ref[...] = wd_ref[...].astype(BF16)

    def run(h_ref, o_ref):
        for c in range(tn // MXU_WIDTH):
            cols = slice(c * MXU_WIDTH, (c + 1) * MXU_WIDTH)

            def one_width(cols=cols):
                h = h_ref[...]
                o_ref[:, cols] = (_silu(_dot(h, wgbf[:, cols])) * _dot(h, wubf[:, cols])).astype(o_ref.dtype)

            pl.when(j >= 0)(one_width)

    _on_streams(functools.partial(run, hp_ref, op_ref), functools.partial(run, hs_ref, os_ref))


def _ride_along_rows(total_rows, n_steps):
    align = 2 * SUBLANES
    return next(r for r in range(align, total_rows + 1, align) if total_rows % r == 0 and total_rows // r <= n_steps)


def mm_gateup(hp, hs, wg, wu, wd, layer, tm=1024, tn=512):
    b, t, k = hp.shape
    ms = hs.shape[0]
    n = wg.shape[2]
    _, kd, nd = wd.shape
    tm = min(tm, t)
    steps, pidx = _streams(b, t, tm)
    n_tiles = pl.cdiv(n, tn)
    tail = n - (n_tiles - 1) * tn
    assert n_tiles >= 2 and tail % LANES == 0 and tn % MXU_WIDTH == 0 and k % CAST_ROWS == 0
    rows = _ride_along_rows(kd, n_tiles * steps)
    wd_row = lambda j, s: jnp.minimum(j * steps + s, kd // rows - 1)
    hbm = pl.BlockSpec(memory_space=pltpu.HBM)
    return pl.pallas_call(
        functools.partial(_gateup_kernel, layer, n_tiles, tail),
        grid=(n_tiles, steps),
        in_specs=[*_stream_specs(pidx, tm, ms, k), hbm, hbm,
                  pl.BlockSpec((None, rows, nd), lambda j, s: (layer, wd_row(j, s), 0))],
        out_specs=[*_stream_specs(pidx, tm, ms, n, tn),
                   pl.BlockSpec((rows, nd), lambda j, s: (wd_row(j, s), 0))],
        out_shape=[jax.ShapeDtypeStruct((b, t, n), BF16), jax.ShapeDtypeStruct((ms, n), BF16),
                   jax.ShapeDtypeStruct((kd, nd), BF16)],
        scratch_shapes=[pltpu.VMEM((k, tn), F32), pltpu.VMEM((k, tn), F32),
                        pltpu.VMEM((k, tn), BF16), pltpu.VMEM((k, tn), BF16),
                        pltpu.SemaphoreType.DMA((2,))],
        compiler_params=_cparams(2),
        name="mm_gateup",
    )(hp, hs, wg, wu, wd)


def _down_kernel(h_ref, w_ref, x_ref, g_ref, o_ref):
    o_ref[...] = x_ref[...] + g_ref[...] * _dot(h_ref[...], w_ref[...])


def mm_down(hid, wbf, x, g, tm_cap=512, tn=512):
    b, t, k = hid.shape
    n = wbf.shape[1]
    tm = _row_tile(t, tm_cap)
    gspec = (pl.BlockSpec((None, 1, tn), lambda i, m, j: (i, 0, j)) if g.shape[1] == 1
             else pl.BlockSpec((None, tm, tn), lambda i, m, j: (i, m, j)))
    return pl.pallas_call(
        _down_kernel,
        grid=(b, t // tm, n // tn),
        in_specs=[pl.BlockSpec((None, tm, k), lambda i, m, j: (i, m, 0)),
                  pl.BlockSpec((k, tn), lambda i, m, j: (0, j)),
                  pl.BlockSpec((None, tm, tn), lambda i, m, j: (i, m, j)),
                  gspec],
        out_specs=pl.BlockSpec((None, tm, tn), lambda i, m, j: (i, m, j)),
        out_shape=jax.ShapeDtypeStruct((b, t, n), F32),
        compiler_params=_cparams(3),
        name="mm_down",
    )(hid, wbf, x, g)


def rope_tables(pos):
    half = ROT_DIM // 2
    inv = ROPE_THETA ** (-jnp.arange(0, ROT_DIM, 2, dtype=F32) / ROT_DIM)
    ang = pos.astype(F32)[:, None] * inv[None, :]
    cos, sin = jnp.cos(ang), jnp.sin(ang)
    t = pos.shape[0]
    rest = HEAD_DIM - ROT_DIM
    c = jnp.concatenate([cos, cos, jnp.ones((t, rest), F32)], axis=1)
    s_up = jnp.concatenate([-sin, jnp.zeros((t, half + rest), F32)], axis=1)
    s_dn = jnp.concatenate([jnp.zeros((t, half), F32), sin, jnp.zeros((t, rest), F32)], axis=1)
    reps = LANES // HEAD_DIM
    return tuple(jnp.tile(a, (1, reps)) for a in (c, s_up, s_dn))


def _rope(x, c, s_up, s_dn):
    w = x.shape[-1]
    reps = w // LANES
    half = ROT_DIM // 2
    tile = lambda a: jnp.concatenate([a] * reps, axis=1) if reps > 1 else a
    return (x * tile(c) + pltpu.roll(x, w - half, 1) * tile(s_up) + pltpu.roll(x, half, 1) * tile(s_dn))


def _low_head(shape):
    return lax.broadcasted_iota(jnp.int32, shape, len(shape) - 1) % LANES < HEAD_DIM


def _swa_prompt_kernel(sinks_ref, q_ref, kv_ref, c_ref, su_ref, sd_ref, o_ref, nk_ref, nv_ref, kprev, vprev):
    n = pl.program_id(1)
    last = pl.num_programs(1) - 1
    blk = q_ref.shape[0]

    @pl.when(n == 0)
    def _():
        kprev[...] = jnp.zeros_like(kprev)
        vprev[...] = jnp.zeros_like(vprev)

    c, su, sd = c_ref[...], su_ref[...], sd_ref[...]
    q = _rope(q_ref[...], c, su, sd) * (HEAD_DIM ** -0.5)
    low = _low_head(q.shape)
    q_heads = (jnp.where(low, q, 0.0).astype(BF16), jnp.where(low, 0.0, q).astype(BF16))
    kv = kv_ref[...]
    k = _rope(kv[:, :KV_W], c, su, sd)
    v = kv[:, KV_W:]

    @pl.when(n == last)
    def _():
        nk_ref[...] = k
        nv_ref[...] = v

    kk = jnp.concatenate([kprev[(n + 1) % 2], k], axis=0)
    vv = jnp.concatenate([vprev[(n + 1) % 2], v], axis=0)
    low2 = _low_head((2 * blk, LANES))
    low1 = _low_head((blk, LANES))
    row = lax.broadcasted_iota(jnp.int32, (blk, blk), 0)
    col = lax.broadcasted_iota(jnp.int32, (blk, blk), 1)
    from_prev = col > row
    no_prev = jnp.where(n > 0, 0.0, -jnp.inf)

    def both_halves(x, g):
        tile = x[:, (g // 2) * LANES:(g // 2 + 1) * LANES]
        other = pltpu.roll(tile, HEAD_DIM, 1)
        return (jnp.where(low2, other, tile) if g % 2 else jnp.where(low2, tile, other)).astype(BF16)

    for g in range(N_KV):
        kd, vd = both_halves(kk, g), both_halves(vv, g)
        for jp in range(Q_PER_KV // 2):
            lanes = slice((g * Q_PER_KV // 2 + jp) * LANES, (g * Q_PER_KV // 2 + jp + 1) * LANES)
            outs = []
            for half in range(2):
                h = g * Q_PER_KV + 2 * jp + half
                s = _dot_nt(q_heads[half][:, lanes], kd)
                s = jnp.where(from_prev, s[:, :blk] + no_prev, s[:, blk:])
                sink = sinks_ref[h]
                m = jnp.maximum(jnp.max(s, axis=-1, keepdims=True), sink)
                e = jnp.exp(s - m)
                den = jnp.sum(e, axis=-1, keepdims=True) + jnp.exp(sink - m)
                pe = jnp.concatenate([jnp.where(from_prev, e, 0.0), jnp.where(from_prev, 0.0, e)], axis=1)
                outs.append(_dot(pe.astype(BF16), vd) * (1.0 / den))
            o_ref[:, lanes] = jnp.where(low1, outs[0], outs[1]).astype(o_ref.dtype)
    kprev[n % 2] = k
    vprev[n % 2] = v


def swa_prompt(proj, sinks, tables):
    b, t, _ = proj.shape
    blk = WINDOW
    nb = t // blk
    kv_col = OFF_K // (2 * KV_W)
    tab_spec = pl.BlockSpec((blk, LANES), lambda i, n: (n, 0))
    keep_spec = pl.BlockSpec((None, blk, KV_W), lambda i, n: (i, 0, 0))
    return pl.pallas_call(
        _swa_prompt_kernel,
        grid=(b, nb),
        in_specs=[pl.BlockSpec(memory_space=pltpu.SMEM),
                  pl.BlockSpec((None, blk, Q_W), lambda i, n: (i, n, 0)),
                  pl.BlockSpec((None, blk, 2 * KV_W), lambda i, n: (i, n, kv_col)),
                  tab_spec, tab_spec, tab_spec],
        out_specs=[pl.BlockSpec((None, blk, Q_W), lambda i, n: (i, n, 0)), keep_spec, keep_spec],
        out_shape=[jax.ShapeDtypeStruct((b, t, Q_W), BF16),
                   jax.ShapeDtypeStruct((b, blk, KV_W), F32),
                   jax.ShapeDtypeStruct((b, blk, KV_W), F32)],
        scratch_shapes=[pltpu.VMEM((2, blk, KV_W), F32), pltpu.VMEM((2, blk, KV_W), F32)],
        compiler_params=_cparams(2),
        name="swa_prompt",
    )(sinks, proj, proj, *tables)


def _swa_decode_kernel(t_new, sinks_ref, q_ref, kv_ref, kc_ref, vc_ref, c_ref, su_ref, sd_ref,
                       o_ref, nk_ref, nv_ref, kk, vv):
    for s in range(q_ref.shape[0]):
        _swa_decode_one(t_new, sinks_ref, q_ref.at[s], kv_ref.at[s], kc_ref.at[s], vc_ref.at[s], c_ref, su_ref,
                        sd_ref, o_ref.at[s], nk_ref.at[s], nv_ref.at[s], kk.at[s], vv.at[s])


def _swa_decode_one(t_new, sinks_ref, q_ref, kv_ref, kc_ref, vc_ref, c_ref, su_ref, sd_ref,
                    o_ref, nk_ref, nv_ref, kk, vv):
    rows = q_ref.shape[0]
    wc = kc_ref.shape[0]
    c, su, sd = c_ref[...], su_ref[...], sd_ref[...]
    q = (_rope(q_ref[...], c, su, sd) * (HEAD_DIM ** -0.5)).astype(BF16)
    kv = kv_ref[...]
    k = _rope(kv[:, :KV_W], c, su, sd)
    v = kv[:, KV_W:]
    kc, vc = kc_ref[...], vc_ref[...]

    nk_ref[0:wc - t_new, :] = kc[t_new:, :]
    nv_ref[0:wc - t_new, :] = vc[t_new:, :]
    nk_ref[wc - t_new:wc, :] = k[0:t_new, :]
    nv_ref[wc - t_new:wc, :] = v[0:t_new, :]

    kk[...] = jnp.zeros_like(kk)
    vv[...] = jnp.zeros_like(vv)
    kk[0:wc, :] = kc.astype(BF16)
    vv[0:wc, :] = vc.astype(BF16)
    kk[wc:wc + rows, :] = k.astype(BF16)
    vv[wc:wc + rows, :] = v.astype(BF16)

    nkeys = kk.shape[0]
    nrow = Q_PER_KV * rows
    r = lax.broadcasted_iota(jnp.int32, (nrow, nkeys), 0)
    sj = lax.broadcasted_iota(jnp.int32, (nrow, nkeys), 1)
    tq = r % rows
    mask = jnp.logical_and(jnp.logical_and(sj <= tq + wc, sj > tq + wc - WINDOW), sj < wc + t_new)
    rcol = lax.broadcasted_iota(jnp.int32, (nrow, 1), 0) // rows
    for g in range(N_KV):
        kg = kk[:, g * HEAD_DIM:(g + 1) * HEAD_DIM]
        vg = vv[:, g * HEAD_DIM:(g + 1) * HEAD_DIM]
        qs = jnp.concatenate([q[:, (g * Q_PER_KV + hh) * HEAD_DIM:(g * Q_PER_KV + hh + 1) * HEAD_DIM]
                              for hh in range(Q_PER_KV)], axis=0)
        sink = jnp.zeros((nrow, 1), F32)
        for hh in range(Q_PER_KV):
            sink = jnp.where(rcol == hh, sinks_ref[g * Q_PER_KV + hh], sink)
        s = jnp.where(mask, _dot_nt(qs, kg), -jnp.inf)
        m = jnp.maximum(jnp.max(s, axis=-1, keepdims=True), sink)
        e = jnp.exp(s - m)
        p = e / (jnp.sum(e, axis=-1, keepdims=True) + jnp.exp(sink - m))
        og = _dot(p.astype(BF16), vg)
        for hh in range(Q_PER_KV):
            h = g * Q_PER_KV + hh
            o_ref[:, h * HEAD_DIM:(h + 1) * HEAD_DIM] = og[hh * rows:(hh + 1) * rows, :].astype(o_ref.dtype)


def swa_decode(proj, sinks, cache_k, cache_v, layer, tables, t_new):
    b, rows, _ = proj.shape
    wc = cache_k.shape[2]
    kv_col = OFF_K // (2 * KV_W)
    per = DECODE_SEQS_PER_STEP if b % DECODE_SEQS_PER_STEP == 0 else 1
    tab_spec = pl.BlockSpec((rows, LANES), lambda i: (0, 0))
    cache_in = pl.BlockSpec((None, per, wc, KV_W), lambda i: (layer, i, 0, 0))
    cache_out = pl.BlockSpec((per, wc, KV_W), lambda i: (i, 0, 0))
    nkeys = 2 * WINDOW
    return pl.pallas_call(
        functools.partial(_swa_decode_kernel, t_new),
        grid=(b // per,),
        in_specs=[pl.BlockSpec(memory_space=pltpu.SMEM),
                  pl.BlockSpec((per, rows, Q_W), lambda i: (i, 0, 0)),
                  pl.BlockSpec((per, rows, 2 * KV_W), lambda i: (i, 0, kv_col)),
                  cache_in, cache_in, tab_spec, tab_spec, tab_spec],
        out_specs=[pl.BlockSpec((per, rows, Q_W), lambda i: (i, 0, 0)), cache_out, cache_out],
        out_shape=[jax.ShapeDtypeStruct((b, rows, Q_W), BF16),
                   jax.ShapeDtypeStruct((b, wc, KV_W), F32),
                   jax.ShapeDtypeStruct((b, wc, KV_W), F32)],
        scratch_shapes=[pltpu.VMEM((per, nkeys, KV_W), BF16), pltpu.VMEM((per, nkeys, KV_W), BF16)],
        compiler_params=_cparams(1),
        name="swa_decode",
    )(sinks, proj, proj, cache_k, cache_v, *tables)


def spread_matrices():
    src = jnp.arange(LANES)
    head = src % N_SSM_HEADS
    live = src < SPLIT_PARTS * N_SSM_HEADS
    to_x = (head[:, None] == (jnp.arange(D_SSM) // SSM_HEAD_DIM)[None, :]) & live[:, None]
    to_seg = (head[:, None] == (jnp.arange(N_SSM_HEADS * CHUNK) // CHUNK)[None, :]) & live[:, None]
    return to_x.astype(BF16), to_seg.astype(BF16)


def _three_copies(x):
    return x + pltpu.roll(x, N_SSM_HEADS, 1) + pltpu.roll(x, 2 * N_SSM_HEADS, 1)


def _split_parts(x3, part):
    hi = x3.astype(BF16).astype(F32)
    r1 = x3 - hi
    mid = r1.astype(BF16).astype(F32)
    lo = r1 - mid
    return jnp.where(part == 0, hi, jnp.where(part == 1, mid, jnp.where(part == 2, lo, 0.0))).astype(BF16)


def _ssd_kernel(valid, proj_ref, cinit_ref, h0_ref, cw_ref, cb_ref, dtb_ref, alog_ref, dsk_ref, nw_ref,
                tox_ref, toseg_ref, o_ref, tail_ref, hn_ref, xp, ybuf):
    cidx = pl.program_id(1)
    qin = proj_ref.shape[0]
    q = ybuf.shape[0]
    pad = q - qin
    hist = SUBLANES

    @pl.when(cidx == 0)
    def _():
        xp[0:hist, :] = cinit_ref[...]
        hn_ref[...] = h0_ref[...]

    blk = proj_ref[...]
    xp[hist:hist + qin, :] = blk[:, OFF_XBC:OFF_DT]
    if pad:
        xp[hist + qin:, :] = jnp.zeros((pad, CONV_DIM), F32)
    tail_ref[...] = xp[valid:valid + hist, :]

    cw = cw_ref[...]
    conv = cb_ref[...] + cw[CONV_W - 1:CONV_W, :] * xp[hist:hist + q, :]
    for j in range(CONV_W - 1):
        shift = CONV_W - 1 - j
        conv = conv + cw[j:j + 1, :] * xp[hist - shift:hist - shift + q, :]
    xc = _silu(conv)
    xp[0:hist, :] = xp[q:q + hist, :]

    xs = xc[:, :D_SSM]
    gw = D_STATE
    bmat = xc[:, D_SSM:D_SSM + N_SSM_GROUPS * gw].astype(BF16)
    cmat = xc[:, D_SSM + N_SSM_GROUPS * gw:].astype(BF16)

    lane = lax.broadcasted_iota(jnp.int32, (q, LANES), 1)
    rowi = lax.broadcasted_iota(jnp.int32, (q, LANES), 0)
    part = lane // N_SSM_HEADS
    dt_raw = blk[:, OFF_DT:OFF_DT + LANES]
    if pad:
        dt_raw = jnp.concatenate([dt_raw, jnp.zeros((pad, LANES), F32)], axis=0)
    dt_raw = _three_copies(jnp.where(lane < N_SSM_HEADS, dt_raw, 0.0))
    xdtb = dt_raw + dtb_ref[...]
    dt = jnp.maximum(xdtb, 0.0) + jnp.log1p(jnp.exp(-jnp.abs(xdtb)))
    dt = jnp.where(jnp.logical_and(rowi < valid, part < SPLIT_PARTS), dt, 0.0)
    a = dt * (-jnp.exp(alog_ref[...]))

    li = lax.broadcasted_iota(jnp.int32, (q, q), 0)
    si = lax.broadcasted_iota(jnp.int32, (q, q), 1)
    causal = li >= si
    tri = causal.astype(BF16)
    triu = (li <= si).astype(BF16)
    pa = _split_parts(a, part)
    c3 = _dot(tri, pa)
    acum = c3
    for k in range(1, LANES // N_SSM_HEADS):
        acum = acum + pltpu.roll(c3, k * N_SSM_HEADS, 1)
    ct = _dot_tn(pa, triu)
    acum_t = ct[0:N_SSM_HEADS] + ct[N_SSM_HEADS:2 * N_SSM_HEADS] + ct[2 * N_SSM_HEADS:3 * N_SSM_HEADS]
    alast = acum[q - 1:q, :]
    e_acum = jnp.exp(acum)
    e_chunk = e_acum[q - 1:q, :]

    tox = tox_ref[...]
    dtx = _dot(_split_parts(dt, part), tox)
    e_end_x = _dot(_split_parts(jnp.exp(alast - acum), part), tox)
    e_acum_x = _dot(_split_parts(e_acum, part), tox)
    pacum = _split_parts(acum, part)

    xdt = xs * dtx
    low = _low_head(xdt.shape)
    xdt_heads = (jnp.where(low, xdt, 0.0).astype(BF16), jnp.where(low, 0.0, xdt).astype(BF16))
    xe = (xdt * e_end_x).astype(BF16)
    ybuf[...] = dsk_ref[...] * xs

    gsz = SSM_R * SSM_HEAD_DIM
    for g in range(N_SSM_GROUPS):
        bg = bmat[:, g * gw:(g + 1) * gw]
        cg = cmat[:, g * gw:(g + 1) * gw]
        glanes = slice(g * gsz, (g + 1) * gsz)
        grows = slice(g * gsz, (g + 1) * gsz)
        cb = _dot_nt(cg, bg)
        hold = hn_ref[grows, :]
        ybuf[:, glanes] += _dot_nt(cg, hold.astype(BF16)) * e_acum_x[:, glanes]
        seg = _dot(pacum, toseg_ref[:, g * SSM_R * q:(g + 1) * SSM_R * q])
        for jp in range(SSM_R // 2):
            ms = []
            for half in range(2):
                r = 2 * jp + half
                h = g * SSM_R + r
                d = seg[:, r * q:(r + 1) * q] - acum_t[h:h + 1, :]
                ms.append((cb * jnp.exp(jnp.where(causal, d, -jnp.inf))).astype(BF16))
            plane = slice(g * gsz + jp * LANES, g * gsz + (jp + 1) * LANES)
            rhs = jnp.concatenate([xdt_heads[0][:, plane], xdt_heads[1][:, plane]], axis=0)
            ybuf[:, plane] += _dot(jnp.concatenate(ms, axis=1), rhs)
        st = _dot_tn(xe[:, glanes], bg)
        for r in range(SSM_R):
            h = g * SSM_R + r
            rows = slice(h * SSM_HEAD_DIM, (h + 1) * SSM_HEAD_DIM)
            hn_ref[rows, :] = (hold[r * SSM_HEAD_DIM:(r + 1) * SSM_HEAD_DIM, :] * e_chunk[:, h:h + 1]
                               + st[r * SSM_HEAD_DIM:(r + 1) * SSM_HEAD_DIM, :])

    z = blk[:, OFF_Z:OFF_XBC]
    gated = ybuf[0:qin, :] * _silu(z)
    nsz = D_SSM // N_SSM_GROUPS
    nw = nw_ref[...]
    for g in range(N_SSM_GROUPS):
        gg = gated[:, g * nsz:(g + 1) * nsz]
        gg = gg * lax.rsqrt(jnp.mean(gg * gg, axis=-1, keepdims=True) + EPS)
        o_ref[:, g * nsz:(g + 1) * nsz] = (gg * nw[:, g * nsz:(g + 1) * nsz]).astype(o_ref.dtype)


def ssd_mixer(proj, conv_init, h0, layer, conv_w, conv_b, dt_bias, a_log, d_skip, norm_w, spread, qin, valid):
    b, t, _ = proj.shape
    nc = t // qin
    copies = lambda a: jnp.concatenate([a] * SPLIT_PARTS + [jnp.zeros_like(a)]).reshape(1, LANES)
    const2 = lambda shape: pl.BlockSpec(shape, lambda i, c: (0, 0))
    state_in = pl.BlockSpec((None, None, STATE_ROWS, D_STATE), lambda i, c: (layer, i, 0, 0))
    state_out = pl.BlockSpec((None, STATE_ROWS, D_STATE), lambda i, c: (i, 0, 0))
    hist_spec = pl.BlockSpec((None, SUBLANES, CONV_DIM), lambda i, c: (i, 0, 0))
    to_x, to_seg = spread
    return pl.pallas_call(
        functools.partial(_ssd_kernel, valid),
        grid=(b, nc),
        in_specs=[pl.BlockSpec((None, qin, P_PAD), lambda i, c: (i, c, 0)),
                  hist_spec, state_in,
                  const2((CONV_W, CONV_DIM)), const2((1, CONV_DIM)), const2((1, LANES)),
                  const2((1, LANES)), const2((1, D_SSM)), const2((1, D_SSM)),
                  const2(to_x.shape), const2(to_seg.shape)],
        out_specs=[pl.BlockSpec((None, qin, D_SSM), lambda i, c: (i, c, 0)), hist_spec, state_out],
        out_shape=[jax.ShapeDtypeStruct((b, t, D_SSM), BF16),
                   jax.ShapeDtypeStruct((b, SUBLANES, CONV_DIM), F32),
                   jax.ShapeDtypeStruct((b, STATE_ROWS, D_STATE), F32)],
        scratch_shapes=[pltpu.VMEM((SUBLANES + CHUNK, CONV_DIM), F32), pltpu.VMEM((CHUNK, D_SSM), F32)],
        compiler_params=_cparams(2),
        name="ssd_mixer",
    )(proj, conv_init, h0, conv_w, conv_b.reshape(1, -1), copies(dt_bias), copies(a_log),
      jnp.repeat(d_skip, SSM_HEAD_DIM).reshape(1, -1), norm_w.reshape(1, -1), to_x, to_seg)


def _layer(xp, xs, mod_p, mod_s, weights, layer, attn_p, attn_s, conv_init_s, h0_s, spread, zeros_p, t_new):
    (norm1_w, w_in_t, conv_w, conv_b, dt_bias, a_log, d_skip, ssm_norm_w, w_out, norm2_w, w_gate, w_up,
     w_down) = weights
    ms = xs.shape[0]
    bs = ms // SAMPLE_ROWS
    as3 = lambda a: a[None]
    hp = norm_mod(xp, norm1_w[layer], mod_p[1], mod_p[0])
    hs = norm_mod(as3(xs), norm1_w[layer], as3(mod_s[1]), as3(mod_s[0]))[0]
    proj_p, proj_s = mm_inproj(hp, hs, w_in_t, layer)
    proj_s = proj_s.reshape(bs, SAMPLE_ROWS, P_PAD)

    ssd = functools.partial(ssd_mixer, conv_w=conv_w[layer], conv_b=conv_b[layer], dt_bias=dt_bias[layer],
                            a_log=a_log[layer], d_skip=d_skip[layer], norm_w=ssm_norm_w[layer], spread=spread)
    oa_p, k_p, v_p = attn_p(proj_p)
    os_p, tail_p, hn_p = ssd(proj_p, zeros_p[0], zeros_p[1], 0, qin=CHUNK, valid=CHUNK)
    oa_s, k_s, v_s = attn_s(proj_s)
    os_s, tail_s, hn_s = ssd(proj_s, conv_init_s, h0_s, layer, qin=SAMPLE_ROWS, valid=t_new)

    xp, xs = mm_outproj((oa_p, oa_s.reshape(ms, Q_W)), (os_p, os_s.reshape(ms, D_SSM)), w_out, layer,
                        (xp, xs), (mod_p[2], mod_s[2]))
    h2p = norm_mod(xp, norm2_w[layer], mod_p[4], mod_p[3])
    h2s = norm_mod(as3(xs), norm2_w[layer], as3(mod_s[4]), as3(mod_s[3]))[0]
    hid_p, hid_s, wdown_bf = mm_gateup(h2p, h2s, w_gate, w_up, w_down, layer)
    xp = mm_down(hid_p, wdown_bf, xp, mod_p[5])
    xs = mm_down(as3(hid_s), wdown_bf, as3(xs), as3(mod_s[5]))[0]

    keep = SUBLANES - (CONV_W - 1)
    state = lambda h: h.reshape(-1, N_SSM_HEADS, SSM_HEAD_DIM, D_STATE)
    heads = lambda a: a.reshape(a.shape[0], a.shape[1], N_KV, HEAD_DIM)
    return xp, xs, (heads(k_p), heads(v_p), tail_p[:, keep:], state(hn_p),
                    heads(k_s), heads(v_s), tail_s[:, keep:], state(hn_s))


def kernel(x_prompt, x_sample, cache_k, cache_v, state_conv, state_ssm, c_prompt, c_sample, w_ada, b_ada,
           norm1_w, w_in, conv_w, conv_b, dt_bias, a_log, d_skip, ssm_norm_w, sinks, w_out, norm2_w,
           w_gate, w_up, w_down, final_norm_w):
    depth = w_ada.shape[0]
    bp, tp, d = x_prompt.shape
    bs, ts, _ = x_sample.shape
    wc = cache_k.shape[2]
    assert ts <= SAMPLE_ROWS and tp % CHUNK == 0 and ts % CHUNK != 0

    c_all = jnp.concatenate([c_prompt, c_sample, jnp.zeros((ADA_ROWS - bp - bs, d), F32)], axis=0)
    mod_all = ada_mod(c_all, w_ada, b_ada)

    tab_p = rope_tables(jnp.arange(tp, dtype=jnp.int32))
    tab_s = rope_tables(PAST_LEN + jnp.arange(SAMPLE_ROWS, dtype=jnp.int32))
    spread = spread_matrices()
    weights = (norm1_w, jnp.swapaxes(w_in, 1, 2), conv_w, conv_b, dt_bias, a_log, d_skip, ssm_norm_w, w_out,
               norm2_w, w_gate, w_up, w_down)

    xp = x_prompt
    xs = jnp.pad(x_sample, ((0, 0), (0, SAMPLE_ROWS - ts), (0, 0))).reshape(bs * SAMPLE_ROWS, d)
    zeros_p = (jnp.zeros((bp, SUBLANES, CONV_DIM), F32), jnp.zeros((1, bp, STATE_ROWS, D_STATE), F32))
    h0_s = state_ssm.reshape(depth, bs, STATE_ROWS, D_STATE)
    ck = cache_k.reshape(depth, bs, wc, KV_W)
    cv = cache_v.reshape(depth, bs, wc, KV_W)

    outs = []
    for l in range(depth):
        mod_p = [mod_all[l, :bp, i * d:(i + 1) * d][:, None, :] for i in range(6)]
        mod_s = [jnp.repeat(mod_all[l, bp:bp + bs, i * d:(i + 1) * d], SAMPLE_ROWS, axis=0) for i in range(6)]
        attn_p = functools.partial(swa_prompt, sinks=sinks[l], tables=tab_p)
        attn_s = functools.partial(swa_decode, sinks=sinks[l], cache_k=ck, cache_v=cv, layer=l, tables=tab_s,
                                   t_new=ts)
        conv_init_s = jnp.pad(state_conv[l], ((0, 0), (SUBLANES - (CONV_W - 1), 0), (0, 0)))
        xp, xs, layer_outs = _layer(xp, xs, mod_p, mod_s, weights, l, attn_p, attn_s, conv_init_s, h0_s, spread,
                                    zeros_p, ts)
        outs.append(layer_outs)

    y_prompt = final_norm(xp, final_norm_w)
    y_sample = final_norm(xs[None], final_norm_w).reshape(bs, SAMPLE_ROWS, d)[:, :ts]
    return (y_prompt, y_sample) + tuple(jnp.stack(o) for o in zip(*outs))
```
